```python
import jax, jax.numpy as jnp
from jax import lax
import numpy as np

D_MODEL = 1024
BATCH = 8
SEQ = 2048
DEPTH = 4
DEC_BATCH = 32
DEC_SEQ = 4
PAST_LEN = 8192
PAGE_SIZE = 128

HEAD_DIM = 64
MIX_W = D_MODEL
CONV_CH = MIX_W // 4
CONV_WIDTH = 31
ATTN_GROUPS = ((128, 1), (512, 4), (2048, 16))
HEADS_PER_GROUP = (MIX_W - CONV_CH) // 2 // HEAD_DIM // len(ATTN_GROUPS)
ATTN_HEADS = HEADS_PER_GROUP * len(ATTN_GROUPS)
ATTN_W = ATTN_HEADS * HEAD_DIM
ATTN_SCALE = HEAD_DIM ** -0.5
RWKV_HEADS = (MIX_W - CONV_CH - ATTN_W) // HEAD_DIM
RWKV_W = RWKV_HEADS * HEAD_DIM
LORA_DECAY = 64
LORA_AAA = 64
LORA_GATE = 128
LNX_EPS = 64e-5
ROPE_THETA = 10000.0
MEM_TOKENS = 256
XATTN_HEADS = 4
XATTN_HEAD_DIM = D_MODEL // XATTN_HEADS
D_FF = 2816
NORM_EPS = 1e-6
A_COLS = 2 * CONV_CH
B_COLS = 3 * ATTN_W
C_COLS = 3 * RWKV_W + LORA_DECAY + LORA_AAA + LORA_GATE
IN_COLS = A_COLS + B_COLS + C_COLS
F32 = jnp.float32

kernel_name = 'hymba_conv_dilated_rwkv7_macaron_step'


def rmsnorm(x, g):
    xf = x.astype(F32)
    y = xf * lax.rsqrt(jnp.mean(xf * xf, axis=-1, keepdims=True) + NORM_EPS)
    return (y * g.astype(F32)).astype(x.dtype)


def swiglu_ffn(h, w_gate, w_up, w_down):
    return (jax.nn.silu(h @ w_gate) * (h @ w_up)) @ w_down


def rope(x, pos):
    half = x.shape[-1] // 2
    inv = jnp.power(ROPE_THETA, -jnp.arange(half, dtype=F32) / half)
    ang = pos.astype(F32)[:, None] * inv[None, :]
    cos = jnp.cos(ang)[None, :, None, :]
    sin = jnp.sin(ang)[None, :, None, :]
    xf = x.astype(F32)
    x1, x2 = xf[..., :half], xf[..., half:]
    return jnp.concatenate([x1 * cos - x2 * sin, x1 * sin + x2 * cos], axis=-1).astype(x.dtype)


def conv_module(u, conv_w, conv_b, ln_g, ln_b):
    y = lax.conv_general_dilated(u, conv_w[:, None, :], window_strides=(1,), padding='VALID',
                                 dimension_numbers=('NWC', 'WIO', 'NWC'),
                                 feature_group_count=CONV_CH) + conv_b
    yf = y.astype(F32)
    mu = jnp.mean(yf, axis=-1, keepdims=True)
    var = jnp.mean(jnp.square(yf - mu), axis=-1, keepdims=True)
    yn = (yf - mu) * lax.rsqrt(var + 1e-5) * ln_g.astype(F32) + ln_b.astype(F32)
    return jax.nn.silu(yn).astype(u.dtype)


def dilated_attn_prompt(q, k, v, window, dil):
    B, S, H, hd = q.shape
    nbk = window // dil
    L = S // dil
    nb = -(-L // nbk)
    Lp = nb * nbk

    def split(x):
        x = x.reshape(B, L, dil, H, hd).transpose(0, 2, 1, 3, 4)
        return jnp.pad(x, ((0, 0), (0, 0), (0, Lp - L), (0, 0), (0, 0)))

    def band(x):
        xp = jnp.pad(x, ((0, 0), (0, 0), (nbk, 0), (0, 0), (0, 0))).reshape(B, dil, nb + 1, nbk, H, hd)
        return jnp.concatenate([xp[:, :, :-1], xp[:, :, 1:]], axis=3)

    qb = split(q).reshape(B, dil, nb, nbk, H, hd)
    kb = band(split(k))
    vb = band(split(v))
    s = jnp.einsum('brnqhd,brnkhd->brnhqk', qb, kb).astype(F32) * ATTN_SCALE
    qi = jnp.arange(nbk)[:, None]
    kj = jnp.arange(2 * nbk)[None, :]
    rel = qi + nbk - kj
    blk = jnp.arange(nb)[:, None, None]
    valid = (rel >= 0) & (rel <= nbk) & (blk * nbk - nbk + kj >= 0)
    s = jnp.where(valid[None, None, :, None], s, -jnp.inf)
    m = jnp.max(s, axis=-1)
    p = jnp.exp(s - m[..., None])
    den = jnp.sum(p, axis=-1)
    o = jnp.einsum('brnhqk,brnkhd->brnqhd', p.astype(v.dtype), vb)

    def back(t):
        t = t.transpose(0, 1, 2, 4, 3).reshape(B, dil, Lp, H)[:, :, :L]
        return t.transpose(0, 2, 1, 3).reshape(B, S, H)

    o = o.reshape(B, dil, Lp, H, hd)[:, :, :L].transpose(0, 2, 1, 3, 4).reshape(B, S, H, hd)
    return o, back(m), back(den)


def dilated_attn_decode(q, kc, vc, window, dil, L):
    T = q.shape[1]
    nk = window // dil + 1
    idx = L + jnp.arange(T)[:, None] - dil * jnp.arange(nk)[None, :]
    valid = idx >= 0
    idx_c = jnp.maximum(idx, 0)
    kg = kc[:, idx_c]
    vg = vc[:, idx_c]
    s = jnp.einsum('bthd,btkhd->bthk', q, kg).astype(F32) * ATTN_SCALE
    s = jnp.where(valid[None, :, None, :], s, -jnp.inf)
    m = jnp.max(s, axis=-1)
    p = jnp.exp(s - m[..., None])
    den = jnp.sum(p, axis=-1)
    o = jnp.einsum('bthk,btkhd->bthd', p.astype(vc.dtype), vg)
    return o, m, den


def combine_dilations(parts):
    ms = jnp.stack([pt[1] for pt in parts])
    dens = jnp.stack([pt[2] for pt in parts])
    m_all = jnp.max(ms, axis=0)
    scl = jnp.exp(ms - m_all)
    wts = scl / jnp.sum(dens * scl, axis=0)
    return jnp.concatenate([pt[0] * wts[g][..., None].astype(pt[0].dtype) for g, pt in enumerate(parts)], axis=2)


def wkv_scan(S0, r, w, k, v, a, b):
    def step(S, inp):
        r_t, w_t, k_t, v_t, a_t, b_t = inp
        Sa = jnp.einsum('bhvk,bhk->bhv', S, a_t)
        S = S * w_t[:, :, None, :] + Sa[..., None] * b_t[:, :, None, :] + v_t[..., None] * k_t[:, :, None, :]
        return S, jnp.einsum('bhvk,bhk->bhv', S, r_t)
    xs = tuple(t.transpose(1, 0, 2, 3) for t in (r, w, k, v, a, b))
    S, o = lax.scan(step, S0, xs)
    return S, o.transpose(1, 0, 2, 3)


def rwkv_time_mix(cols, prev, S0, shift_mix, w0, w_up, a0, a_up, g_up, k_k, k_a, r_k, lnx_g, lnx_b):
    B, T, _ = cols.shape
    f = cols.astype(F32)
    shifted = jnp.concatenate([prev.astype(F32)[:, None], f[:, :-1]], axis=1)
    xs = f + (shifted - f) * shift_mix.astype(F32)
    r, k, v, wlo, alo, glo = jnp.split(xs, [RWKV_W, 2 * RWKV_W, 3 * RWKV_W, 3 * RWKV_W + LORA_DECAY,
                                            3 * RWKV_W + LORA_DECAY + LORA_AAA], axis=-1)
    w = -jax.nn.softplus(-(w0.astype(F32) + jnp.tanh(wlo) @ w_up.astype(F32))) - 0.5
    decay = jnp.exp(-jnp.exp(w))
    a = jax.nn.sigmoid(a0.astype(F32) + alo @ a_up.astype(F32))
    g = jax.nn.sigmoid(glo) @ g_up.astype(F32)

    def heads(t):
        return t.reshape(B, T, RWKV_HEADS, HEAD_DIM)

    kk = heads(k * k_k.astype(F32))
    kk = kk / jnp.maximum(jnp.sqrt(jnp.sum(kk * kk, axis=-1, keepdims=True)), 1e-12)
    k = k * (1.0 + (a - 1.0) * k_a.astype(F32))
    r_h, k_h, v_h, w_h, a_h = heads(r), heads(k), heads(v), heads(decay), heads(a)
    S, o = wkv_scan(S0.astype(F32), r_h, w_h, k_h, v_h, -kk, kk * a_h)
    mu = jnp.mean(o, axis=-1, keepdims=True)
    var = jnp.mean(jnp.square(o - mu), axis=-1, keepdims=True)
    o = ((o - mu) * lax.rsqrt(var + LNX_EPS)).reshape(B, T, RWKV_W) * lnx_g.astype(F32) + lnx_b.astype(F32)
    bonus = jnp.sum(r_h * k_h * r_k.astype(F32), axis=-1, keepdims=True) * v_h
    o = (o + bonus.reshape(B, T, RWKV_W)) * g
    return o.astype(cols.dtype), S


def cross_attend(h, mk, mv, w_q, w_o):
    B, T, D = h.shape
    q = (h @ w_q).reshape(B, T, XATTN_HEADS, XATTN_HEAD_DIM)
    s = jnp.einsum('bthd,bmhd->bhtm', q, mk).astype(F32) * (XATTN_HEAD_DIM ** -0.5)
    p = jax.nn.softmax(s, axis=-1).astype(mv.dtype)
    o = jnp.einsum('bhtm,bmhd->bthd', p, mv).reshape(B, T, D)
    return o @ w_o


def run_trunk(x, pos, P, mem, mem_k_cache, mem_v_cache, conv_cache, win_caches, shift_cache, wkv_cache):
    prompt = mem is not None
    B, T, D = x.shape
    n_groups = len(ATTN_GROUPS)
    conv_new, shift_new, wkv_new, memk_new, memv_new = [], [], [], [], []
    wink_new = [[] for _ in range(n_groups)]
    winv_new = [[] for _ in range(n_groups)]
    for l in range(DEPTH):
        h = rmsnorm(x, P['ffn1_norm'][l])
        x = x + 0.5 * swiglu_ffn(h, P['ffn1_w_gate'][l], P['ffn1_w_up'][l], P['ffn1_w_down'][l])

        h = rmsnorm(x, P['mix_norm'][l])
        cols = h @ P['w_in'][l]
        cols_a = cols[..., :A_COLS]
        cols_b = cols[..., A_COLS:A_COLS + B_COLS]
        cols_c = cols[..., A_COLS + B_COLS:]

        glu = cols_a[..., :CONV_CH] * jax.nn.sigmoid(cols_a[..., CONV_CH:])
        hist = jnp.zeros((B, CONV_WIDTH - 1, CONV_CH), x.dtype) if prompt else conv_cache[l]
        u = jnp.concatenate([hist, glu], axis=1)
        y_a = conv_module(u, P['conv_w'][l], P['conv_b'][l], P['conv_ln_g'][l], P['conv_ln_b'][l])
        conv_new.append(u[:, -(CONV_WIDTH - 1):])

        q, k, v = (t.reshape(B, T, ATTN_HEADS, HEAD_DIM) for t in jnp.split(cols_b, 3, axis=-1))
        q = rope(q, pos)
        k = rope(k, pos)
        parts = []
        for g, (win, dil) in enumerate(ATTN_GROUPS):
            hs = slice(g * HEADS_PER_GROUP, (g + 1) * HEADS_PER_GROUP)
            qg, kg, vg = q[:, :, hs], k[:, :, hs], v[:, :, hs]
            if prompt:
                parts.append(dilated_attn_prompt(qg, kg, vg, win, dil))
                keep = min(win, T)
                wink_new[g].append(kg[:, T - keep:])
                winv_new[g].append(vg[:, T - keep:])
            else:
                kbuf, vbuf = win_caches[g][0][l], win_caches[g][1][l]
                L = kbuf.shape[1]
                kc = jnp.concatenate([kbuf, kg], axis=1)
                vc = jnp.concatenate([vbuf, vg], axis=1)
                parts.append(dilated_attn_decode(qg, kc, vc, win, dil, L))
                wink_new[g].append(kc[:, -L:])
                winv_new[g].append(vc[:, -L:])
        y_b = combine_dilations(parts).reshape(B, T, ATTN_W)

        prev = jnp.zeros((B, C_COLS), x.dtype) if prompt else shift_cache[l]
        S0 = jnp.zeros((B, RWKV_HEADS, HEAD_DIM, HEAD_DIM), F32) if prompt else wkv_cache[l]
        y_c, S = rwkv_time_mix(cols_c, prev, S0, P['rwkv_shift_mix'][l], P['rwkv_w0'][l], P['rwkv_w_up'][l],
                               P['rwkv_a0'][l], P['rwkv_a_up'][l], P['rwkv_g_up'][l], P['rwkv_k_k'][l],
                               P['rwkv_k_a'][l], P['rwkv_r_k'][l], P['rwkv_lnx_g'][l], P['rwkv_lnx_b'][l])
        shift_new.append(cols_c[:, -1])
        wkv_new.append(S.astype(x.dtype))

        x = x + jnp.concatenate([y_a, y_b, y_c], axis=-1) @ P['w_out'][l]

        h = rmsnorm(x, P['xattn_norm'][l])
        if prompt:
            hm = rmsnorm(mem, P['mem_norm'][l])
            mk, mv = jnp.split(hm @ P['xattn_w_kv'][l], 2, axis=-1)
            mk = mk.reshape(B, mem.shape[1], XATTN_HEADS, XATTN_HEAD_DIM)
            mv = mv.reshape(B, mem.shape[1], XATTN_HEADS, XATTN_HEAD_DIM)
            memk_new.append(mk)
            memv_new.append(mv)
        else:
            mk, mv = mem_k_cache[l], mem_v_cache[l]
        x = x + cross_attend(h, mk, mv, P['xattn_w_q'][l], P['xattn_w_o'][l])

        h = rmsnorm(x, P['ffn2_norm'][l])
        x = x + 0.5 * swiglu_ffn(h, P['ffn2_w_gate'][l], P['ffn2_w_up'][l], P['ffn2_w_down'][l])

    y = rmsnorm(x, P['final_norm'])
    memk = jnp.stack(memk_new) if prompt else None
    memv = jnp.stack(memv_new) if prompt else None
    return (y, jnp.stack(conv_new),
            jnp.stack(wink_new[0]), jnp.stack(winv_new[0]),
            jnp.stack(wink_new[1]), jnp.stack(winv_new[1]),
            jnp.stack(wink_new[2]), jnp.stack(winv_new[2]),
            jnp.stack(shift_new), jnp.stack(wkv_new), memk, memv)


def setup_inputs(seed: int = 0) -> dict:
    key = jax.random.key(seed)
    keys = iter(jax.random.split(key, 64))

    def nrm(shape, scale):
        return scale * jax.random.normal(next(keys), shape, F32)

    def gain(shape):
        return 1.0 + 0.05 * jax.random.normal(next(keys), shape, F32)

    D = D_MODEL
    inp = {}
    inp['x_prompt'] = nrm((BATCH, SEQ, D), 1.0)
    inp['x_sample'] = nrm((DEC_BATCH, DEC_SEQ, D), 1.0)
    inp['cache_conv'] = nrm((DEPTH, DEC_BATCH, CONV_WIDTH - 1, CONV_CH), 0.5)
    for g, (win, dil) in enumerate(ATTN_GROUPS):
        L = min(win, PAST_LEN)
        inp['cache_win%d_k' % (g + 1)] = nrm((DEPTH, DEC_BATCH, L, HEADS_PER_GROUP, HEAD_DIM), 1.0)
        inp['cache_win%d_v' % (g + 1)] = nrm((DEPTH, DEC_BATCH, L, HEADS_PER_GROUP, HEAD_DIM), 1.0)
    inp['state_shift'] = nrm((DEPTH, DEC_BATCH, C_COLS), 1.0)
    inp['state_wkv'] = nrm((DEPTH, DEC_BATCH, RWKV_HEADS, HEAD_DIM, HEAD_DIM), 0.1)
    inp['cache_mem_k'] = nrm((DEPTH, DEC_BATCH, MEM_TOKENS, XATTN_HEADS, XATTN_HEAD_DIM), 1.0)
    inp['cache_mem_v'] = nrm((DEPTH, DEC_BATCH, MEM_TOKENS, XATTN_HEADS, XATTN_HEAD_DIM), 1.0)
    inp['mem_prompt'] = nrm((BATCH, MEM_TOKENS, D), 1.0)

    inp['ffn1_norm'] = gain((DEPTH, D))
    inp['ffn1_w_gate'] = nrm((DEPTH, D, D_FF), D ** -0.5)
    inp['ffn1_w_up'] = nrm((DEPTH, D, D_FF), D ** -0.5)
    inp['ffn1_w_down'] = nrm((DEPTH, D_FF, D), D_FF ** -0.5)
    inp['mix_norm'] = gain((DEPTH, D))
    inp['w_in'] = nrm((DEPTH, D, IN_COLS), D ** -0.5)
    inp['w_out'] = nrm((DEPTH, MIX_W, D), MIX_W ** -0.5)
    inp['conv_w'] = nrm((DEPTH, CONV_WIDTH, CONV_CH), CONV_WIDTH ** -0.5)
    inp['conv_b'] = nrm((DEPTH, CONV_CH), 0.02)
    inp['conv_ln_g'] = gain((DEPTH, CONV_CH))
    inp['conv_ln_b'] = nrm((DEPTH, CONV_CH), 0.02)
    inp['rwkv_shift_mix'] = jax.random.uniform(next(keys), (DEPTH, C_COLS), F32)
    inp['rwkv_w0'] = -1.0 + nrm((DEPTH, RWKV_W), 0.5)
    inp['rwkv_w_up'] = nrm((DEPTH, LORA_DECAY, RWKV_W), LORA_DECAY ** -0.5)
    inp['rwkv_a0'] = nrm((DEPTH, RWKV_W), 0.5)
    inp['rwkv_a_up'] = nrm((DEPTH, LORA_AAA, RWKV_W), LORA_AAA ** -0.5)
    inp['rwkv_g_up'] = nrm((DEPTH, LORA_GATE, RWKV_W), LORA_GATE ** -0.5)
    inp['rwkv_k_k'] = 0.85 + nrm((DEPTH, RWKV_W), 0.1)
    inp['rwkv_k_a'] = 1.0 + nrm((DEPTH, RWKV_W), 0.1)
    inp['rwkv_r_k'] = nrm((DEPTH, RWKV_HEADS, HEAD_DIM), 0.1)
    inp['rwkv_lnx_g'] = gain((DEPTH, RWKV_W))
    inp['rwkv_lnx_b'] = nrm((DEPTH, RWKV_W), 0.02)
    inp['xattn_norm'] = gain((DEPTH, D))
    inp['mem_norm'] = gain((DEPTH, D))
    inp['xattn_w_q'] = nrm((DEPTH, D, D), D ** -0.5)
    inp['xattn_w_kv'] = nrm((DEPTH, D, 2 * D), D ** -0.5)
    inp['xattn_w_o'] = nrm((DEPTH, D, D), D ** -0.5)
    inp['ffn2_norm'] = gain((DEPTH, D))
    inp['ffn2_w_gate'] = nrm((DEPTH, D, D_FF), D ** -0.5)
    inp['ffn2_w_up'] = nrm((DEPTH, D, D_FF), D ** -0.5)
    inp['ffn2_w_down'] = nrm((DEPTH, D_FF, D), D_FF ** -0.5)
    inp['final_norm'] = gain((D,))
    return inp


def reference(x_prompt, x_sample, cache_conv, cache_win1_k, cache_win1_v, cache_win2_k, cache_win2_v,
              cache_win3_k, cache_win3_v, state_shift, state_wkv, cache_mem_k, cache_mem_v, mem_prompt,
              ffn1_norm, ffn1_w_gate, ffn1_w_up, ffn1_w_down, mix_norm, w_in, w_out,
              conv_w, conv_b, conv_ln_g, conv_ln_b,
              rwkv_shift_mix, rwkv_w0, rwkv_w_up, rwkv_a0, rwkv_a_up, rwkv_g_up, rwkv_k_k, rwkv_k_a,
              rwkv_r_k, rwkv_lnx_g, rwkv_lnx_b,
              xattn_norm, mem_norm, xattn_w_q, xattn_w_kv, xattn_w_o,
              ffn2_norm, ffn2_w_gate, ffn2_w_up, ffn2_w_down, final_norm):
    P = dict(ffn1_norm=ffn1_norm, ffn1_w_gate=ffn1_w_gate, ffn1_w_up=ffn1_w_up, ffn1_w_down=ffn1_w_down,
             mix_norm=mix_norm, w_in=w_in, w_out=w_out,
             conv_w=conv_w, conv_b=conv_b, conv_ln_g=conv_ln_g, conv_ln_b=conv_ln_b,
             rwkv_shift_mix=rwkv_shift_mix, rwkv_w0=rwkv_w0, rwkv_w_up=rwkv_w_up, rwkv_a0=rwkv_a0,
             rwkv_a_up=rwkv_a_up, rwkv_g_up=rwkv_g_up, rwkv_k_k=rwkv_k_k, rwkv_k_a=rwkv_k_a,
             rwkv_r_k=rwkv_r_k, rwkv_lnx_g=rwkv_lnx_g, rwkv_lnx_b=rwkv_lnx_b,
             xattn_norm=xattn_norm, mem_norm=mem_norm, xattn_w_q=xattn_w_q, xattn_w_kv=xattn_w_kv,
             xattn_w_o=xattn_w_o, ffn2_norm=ffn2_norm, ffn2_w_gate=ffn2_w_gate, ffn2_w_up=ffn2_w_up,
             ffn2_w_down=ffn2_w_down, final_norm=final_norm)

    pos_p = jnp.arange(x_prompt.shape[1], dtype=jnp.int32)
    (y_prompt, conv_p, win1_k_p, win1_v_p, win2_k_p, win2_v_p, win3_k_p, win3_v_p,
     shift_p, wkv_p, mem_k_p, mem_v_p) = run_trunk(x_prompt, pos_p, P, mem_prompt, None, None,
                                                   None, None, None, None)

    pos_s = PAST_LEN + jnp.arange(x_sample.shape[1], dtype=jnp.int32)
    win_caches = [(cache_win1_k, cache_win1_v), (cache_win2_k, cache_win2_v), (cache_win3_k, cache_win3_v)]
    (y_sample, conv_s, win1_k_s, win1_v_s, win2_k_s, win2_v_s, win3_k_s, win3_v_s,
     shift_s, wkv_s, _unused_k, _unused_v) = run_trunk(x_sample, pos_s, P, None, cache_mem_k, cache_mem_v,
                                                       cache_conv, win_caches, state_shift, state_wkv)

    return (y_prompt, y_sample,
            conv_p, win1_k_p, win1_v_p, win2_k_p, win2_v_p, win3_k_p, win3_v_p, shift_p, wkv_p, mem_k_p, mem_v_p,
            conv_s, win1_k_s, win1_v_s, win2_k_s, win2_v_s, win3_k_s, win3_v_s, shift_s, wkv_s)
```

```python
import functools

import jax
import jax.numpy as jnp
from jax import lax
from jax.experimental import pallas as pl
from jax.experimental.pallas import tpu as pltpu

F32 = jnp.float32
BF16 = jnp.bfloat16

D_MODEL = 1024
DEPTH = 4
HEAD_DIM = 64
CONV_CH = 256
CONV_WIDTH = 31
ATTN_GROUPS = ((128, 1), (512, 4), (2048, 16))
GROUP_W = 2 * HEAD_DIM
ATTN_W = GROUP_W * len(ATTN_GROUPS)
ATTN_SCALE = HEAD_DIM ** -0.5
ATTN_BLOCK = 128
RWKV_HEADS = 6
RWKV_W = RWKV_HEADS * HEAD_DIM
LORA_DECAY = 64
LORA_AAA = 64
LORA_GATE = 128
LNX_EPS = 64e-5
ROPE_THETA = 10000.0
XATTN_HEADS = 4
XATTN_HEAD_DIM = D_MODEL // XATTN_HEADS
NORM_EPS = 1e-6
A_COLS = 2 * CONV_CH
B_COLS = 3 * ATTN_W
C_COLS = 3 * RWKV_W + LORA_DECAY + LORA_AAA + LORA_GATE
PAST_LEN = 8192

VMEM_LIMIT_BYTES = 56 * 1024 * 1024
FFN_TF = 256
RWKV_CHUNK = 64
SUBLANES = 8


def _cparams(*sem):
    return pltpu.CompilerParams(dimension_semantics=sem, vmem_limit_bytes=VMEM_LIMIT_BYTES)


def _resident(shape):
    nd = len(shape)
    return pl.BlockSpec(shape, lambda *_: (0,) * nd, pipeline_mode=pl.Buffered(1))


def _rms(x, g):
    return x * lax.rsqrt(jnp.mean(x * x, axis=-1, keepdims=True) + NORM_EPS) * g


def _dot(a, b):
    return jnp.dot(a, b, preferred_element_type=F32)


def _dot_nt(a, b):
    return lax.dot_general(a, b, (((1,), (1,)), ((), ())), preferred_element_type=F32)


def _row_tile(m, target):
    t = min(m, target)
    assert m % t == 0, (m, t)
    return t


def _ffn_body(x_ref, g_ref, wg_ref, wu_ref, wd_ref, o_ref):
    x = x_ref[...]
    h = _rms(x, g_ref[...]).astype(BF16)
    acc = jnp.zeros(x.shape, F32)
    for j in range(wg_ref.shape[1] // FFN_TF):
        sl = slice(j * FFN_TF, (j + 1) * FFN_TF)
        gate = _dot(h, wg_ref[:, sl])
        up = _dot(h, wu_ref[:, sl])
        act = (gate * jax.nn.sigmoid(gate) * up).astype(BF16)
        acc = acc + _dot(act, wd_ref[sl, :])
    o_ref[...] = x + 0.5 * acc


def _ffn(x, g, wg, wu, wd):
    m, d = x.shape
    tm = _row_tile(m, 512)
    row = pl.BlockSpec((tm, d), lambda i: (i, 0))
    return pl.pallas_call(
        _ffn_body, grid=(m // tm,),
        in_specs=[row, _resident((1, d)), _resident(wg.shape), _resident(wu.shape), _resident(wd.shape)],
        out_specs=row, out_shape=jax.ShapeDtypeStruct((m, d), F32),
        compiler_params=_cparams("parallel"), name="ffn",
    )(x, g.reshape(1, d), wg, wu, wd)


def _norm_proj_body(x_ref, g_ref, w_ref, *o_refs):
    h = _rms(x_ref[...], g_ref[...]).astype(BF16)
    off = 0
    for o_ref in o_refs:
        n = o_ref.shape[1]
        o_ref[...] = _dot(h, w_ref[:, off:off + n])
        off += n


def _norm_proj(x, g, w, splits):
    m, d = x.shape
    tm = _row_tile(m, 512)
    return pl.pallas_call(
        _norm_proj_body, grid=(m // tm,),
        in_specs=[pl.BlockSpec((tm, d), lambda i: (i, 0)), _resident((1, d)), _resident(w.shape)],
        out_specs=[pl.BlockSpec((tm, n), lambda i: (i, 0)) for n in splits],
        out_shape=[jax.ShapeDtypeStruct((m, n), F32) for n in splits],
        compiler_params=_cparams("parallel"), name="norm_proj",
    )(x, g.reshape(1, d), w)


def _out_proj_body(x_ref, *refs):
    y_refs, w_ref, o_ref = refs[:-2], refs[-2], refs[-1]
    acc = x_ref[...]
    off = 0
    for y_ref in y_refs:
        n = y_ref.shape[1]
        acc = acc + _dot(y_ref[...].astype(BF16), w_ref[off:off + n, :])
        off += n
    o_ref[...] = acc


def _out_proj(x, ys, w):
    m, d = x.shape
    tm = _row_tile(m, 512)
    row = lambda n: pl.BlockSpec((tm, n), lambda i: (i, 0))
    return pl.pallas_call(
        _out_proj_body, grid=(m // tm,),
        in_specs=[row(d)] + [row(y.shape[1]) for y in ys] + [_resident(w.shape)],
        out_specs=row(d), out_shape=jax.ShapeDtypeStruct((m, d), F32),
        compiler_params=_cparams("parallel"), name="out_proj",
    )(x, *ys, w)


CONV_HIST = CONV_WIDTH - 1
CONV_PAD = 32


def _conv_body(ca_ref, hist_ref, w_ref, b_ref, lg_ref, lb_ref, y_ref, st_ref, u_scr, *, chunk):
    t = ca_ref.shape[1]
    x = ca_ref[0]
    u_scr[CONV_PAD - CONV_HIST:CONV_PAD, :] = hist_ref[0]
    u_scr[CONV_PAD:CONV_PAD + t, :] = x[:, :CONV_CH] * jax.nn.sigmoid(x[:, CONV_CH:])
    st_ref[0] = u_scr[CONV_PAD + t - CONV_HIST:CONV_PAD + t, :]
    base = CONV_PAD - CONV_HIST
    for c in range(t // chunk):
        acc = jnp.zeros((chunk, CONV_CH), F32) + b_ref[...]
        for j in range(CONV_WIDTH):
            r0 = c * chunk + base + j
            acc = acc + u_scr[r0:r0 + chunk, :] * w_ref[j:j + 1, :]
        mu = jnp.mean(acc, axis=-1, keepdims=True)
        var = jnp.mean(jnp.square(acc - mu), axis=-1, keepdims=True)
        yn = (acc - mu) * lax.rsqrt(var + 1e-5) * lg_ref[...] + lb_ref[...]
        y_ref[0, c * chunk:(c + 1) * chunk, :] = yn * jax.nn.sigmoid(yn)


def _conv_mixer(cols_a, hist, w, b, lg, lb):
    bsz, t, _ = cols_a.shape
    chunk = _row_tile(t, 128)
    vec = lambda a: a.reshape(1, CONV_CH)
    return pl.pallas_call(
        functools.partial(_conv_body, chunk=chunk), grid=(bsz,),
        in_specs=[pl.BlockSpec((1, t, A_COLS), lambda i: (i, 0, 0)),
                  pl.BlockSpec((1, CONV_HIST, CONV_CH), lambda i: (i, 0, 0)),
                  _resident((CONV_WIDTH, CONV_CH)), _resident((1, CONV_CH)),
                  _resident((1, CONV_CH)), _resident((1, CONV_CH))],
        out_specs=[pl.BlockSpec((1, t, CONV_CH), lambda i: (i, 0, 0)),
                   pl.BlockSpec((1, CONV_HIST, CONV_CH), lambda i: (i, 0, 0))],
        out_shape=[jax.ShapeDtypeStruct((bsz, t, CONV_CH), F32),
                   jax.ShapeDtypeStruct((bsz, CONV_HIST, CONV_CH), F32)],
        scratch_shapes=[pltpu.VMEM((CONV_PAD + t, CONV_CH), F32)],
        compiler_params=_cparams("parallel"), name="conv_mixer",
    )(cols_a, hist, w, vec(b), vec(lg), vec(lb))


def _rope_tables(pos):
    half = HEAD_DIM // 2
    inv = jnp.power(ROPE_THETA, -jnp.arange(half, dtype=F32) / half)
    ang = pos.astype(F32)[:, None] * inv[None, :]
    cos, sin = jnp.cos(ang), jnp.sin(ang)
    cos_t = jnp.concatenate([cos, cos], axis=-1)
    sin_t = jnp.concatenate([-sin, sin], axis=-1)
    reps = GROUP_W // HEAD_DIM
    return jnp.tile(cos_t, (1, reps)), jnp.tile(sin_t, (1, reps))


def _rope(x, cos, sin):
    half = HEAD_DIM // 2
    lane = lax.broadcasted_iota(jnp.int32, x.shape, 1)
    first_half = (lane & (HEAD_DIM - 1)) < half
    partner = jnp.where(first_half, pltpu.roll(x, GROUP_W - half, 1), pltpu.roll(x, half, 1))
    return x * cos + partner * sin


def _softmax_parts(scores):
    m = scores[0].max(axis=-1, keepdims=True)
    for s in scores[1:]:
        m = jnp.maximum(m, s.max(axis=-1, keepdims=True))
    ps = [jnp.exp(s - m) for s in scores]
    den = ps[0].sum(axis=-1, keepdims=True)
    for p in ps[1:]:
        den = den + p.sum(axis=-1, keepdims=True)
    return m, ps, den


def _lanes_per_head(vals):
    return jnp.concatenate([jnp.broadcast_to(v, (v.shape[0], HEAD_DIM)) for v in vals], axis=1)


def _attn_prompt_body(qk_ref, v1_ref, v2_ref, v3_ref, cos_ref, sin_ref,
                      y1_ref, y2_ref, y3_ref, k1_ref, k2_ref, k3_ref,
                      q_scr, pq_scr, pk_scr, pv_scr, po_scr, pm_scr, pd_scr, m_scr, d_scr):
    v_refs, y_refs, kr_refs = (v1_ref, v2_ref, v3_ref), (y1_ref, y2_ref, y3_ref), (k1_ref, k2_ref, k3_ref)
    ng = len(ATTN_GROUPS)
    t = qk_ref.shape[1]
    rc = min(t, 256)
    nblk = t // ATTN_BLOCK

    def rope_chunk(i, carry):
        rows = pl.ds(pl.multiple_of(i * rc, rc), rc)
        cs, sn = cos_ref[rows, :], sin_ref[rows, :]
        for g in range(ng):
            q_scr[g, rows, :] = _rope(qk_ref[0, rows, g * GROUP_W:(g + 1) * GROUP_W], cs, sn) * ATTN_SCALE
            kr_refs[g][0, rows, :] = _rope(qk_ref[0, rows, ATTN_W + g * GROUP_W:ATTN_W + (g + 1) * GROUP_W], cs, sn)
        return carry

    lax.fori_loop(0, t // rc, rope_chunk, 0)

    qi = lax.broadcasted_iota(jnp.int32, (ATTN_BLOCK, ATTN_BLOCK), 0)
    kj = lax.broadcasted_iota(jnp.int32, (ATTN_BLOCK, ATTN_BLOCK), 1)

    for g, (win, dil) in enumerate(ATTN_GROUPS):
        assert win // dil == ATTN_BLOCK and t % (dil * ATTN_BLOCK) == 0
        sub_len = t // dil
        sub_blocks = sub_len // ATTN_BLOCK

        for r in range(dil):
            rows = pl.ds(r, sub_len, stride=dil) if dil > 1 else slice(0, t)
            dst = slice(r * sub_len, (r + 1) * sub_len)
            pq_scr[dst, :] = q_scr[g, rows, :].astype(BF16)
            pk_scr[dst, :] = kr_refs[g][0, rows, :].astype(BF16)
            pv_scr[dst, :] = v_refs[g][0, rows, :].astype(BF16)

        def block(i, carry):
            own = pl.ds(pl.multiple_of(i * ATTN_BLOCK, ATTN_BLOCK), ATTN_BLOCK)
            q = pq_scr[own, :]
            k_own, v_own = pk_scr[own, :], pv_scr[own, :]
            masks = [kj <= qi]
            if sub_blocks > 1:
                prev = pl.ds(pl.multiple_of(jnp.maximum(i - 1, 0) * ATTN_BLOCK, ATTN_BLOCK), ATTN_BLOCK)
                k_prev, v_prev = pk_scr[prev, :], pv_scr[prev, :]
                has_prev = lax.rem(i, sub_blocks) > 0
                masks.append(jnp.logical_and(kj >= qi, has_prev))
            os, ms, ds = [], [], []
            for h in range(GROUP_W // HEAD_DIM):
                hl = slice(h * HEAD_DIM, (h + 1) * HEAD_DIM)
                ks, vs = [k_own[:, hl]], [v_own[:, hl]]
                if sub_blocks > 1:
                    ks.append(k_prev[:, hl])
                    vs.append(v_prev[:, hl])
                scores = [jnp.where(mk, _dot_nt(q[:, hl], kk), -jnp.inf) for mk, kk in zip(masks, ks)]
                m, ps, den = _softmax_parts(scores)
                o = _dot(ps[0].astype(BF16), vs[0])
                for p, vv in zip(ps[1:], vs[1:]):
                    o = o + _dot(p.astype(BF16), vv)
                os.append(o)
                ms.append(m)
                ds.append(den)
            po_scr[own, :] = jnp.concatenate(os, axis=1)
            pm_scr[own, :] = _lanes_per_head(ms)
            pd_scr[own, :] = _lanes_per_head(ds)
            return carry

        lax.fori_loop(0, nblk, block, 0)

        for r in range(dil):
            rows = pl.ds(r, sub_len, stride=dil) if dil > 1 else slice(0, t)
            src = slice(r * sub_len, (r + 1) * sub_len)
            y_refs[g][0, rows, :] = po_scr[src, :]
            m_scr[g, rows, :] = pm_scr[src, :]
            d_scr[g, rows, :] = pd_scr[src, :]

    def combine_chunk(i, carry):
        rows = pl.ds(pl.multiple_of(i * rc, rc), rc)
        ms = [m_scr[g, rows, :] for g in range(ng)]
        m_all = functools.reduce(jnp.maximum, ms)
        scl = [jnp.exp(m - m_all) for m in ms]
        tot = functools.reduce(lambda a, b: a + b, [d_scr[g, rows, :] * scl[g] for g in range(ng)])
        for g in range(ng):
            y_refs[g][0, rows, :] = y_refs[g][0, rows, :] * (scl[g] / tot)
        return carry

    lax.fori_loop(0, t // rc, combine_chunk, 0)


def _attn_prompt(qk, vs, cos_t, sin_t):
    bsz, t, _ = qk.shape
    ng = len(ATTN_GROUPS)
    grp = pl.BlockSpec((1, t, GROUP_W), lambda i: (i, 0, 0))
    grp_in = pl.BlockSpec((1, t, GROUP_W), lambda i: (i, 0, 0), pipeline_mode=pl.Buffered(1))
    outs = pl.pallas_call(
        _attn_prompt_body, grid=(bsz,),
        in_specs=[pl.BlockSpec((1, t, 2 * ATTN_W), lambda i: (i, 0, 0), pipeline_mode=pl.Buffered(1))]
                 + [grp_in] * ng + [_resident((t, GROUP_W)), _resident((t, GROUP_W))],
        out_specs=[grp] * (2 * ng),
        out_shape=[jax.ShapeDtypeStruct((bsz, t, GROUP_W), F32)] * (2 * ng),
        scratch_shapes=[pltpu.VMEM((ng, t, GROUP_W), F32),
                        pltpu.VMEM((t, GROUP_W), BF16), pltpu.VMEM((t, GROUP_W), BF16),
                        pltpu.VMEM((t, GROUP_W), BF16),
                        pltpu.VMEM((t, GROUP_W), F32), pltpu.VMEM((t, GROUP_W), F32),
                        pltpu.VMEM((t, GROUP_W), F32),
                        pltpu.VMEM((ng, t, GROUP_W), F32), pltpu.VMEM((ng, t, GROUP_W), F32)],
        compiler_params=_cparams("parallel"), name="attn_prompt",
    )(qk, *vs, cos_t, sin_t)
    return outs[:ng], outs[ng:]


def _attn_decode_body(qk_ref, v1_ref, v2_ref, v3_ref, cos_ref, sin_ref, *refs, t_new):
    ng = len(ATTN_GROUPS)
    v_refs = (v1_ref, v2_ref, v3_ref)
    cache_refs, y_ref, out_refs = refs[:2 * ng], refs[2 * ng], refs[2 * ng + 1:]
    tp = qk_ref.shape[1]
    cs, sn = cos_ref[...], sin_ref[...]
    parts = []
    for g, (win, dil) in enumerate(ATTN_GROUPS):
        assert dil & (dil - 1) == 0
        kb_ref, vb_ref = cache_refs[2 * g], cache_refs[2 * g + 1]
        ko_ref, vo_ref = out_refs[2 * g], out_refs[2 * g + 1]
        n_old = kb_ref.shape[1]
        q = (_rope(qk_ref[0, :, g * GROUP_W:(g + 1) * GROUP_W], cs, sn) * ATTN_SCALE).astype(BF16)
        k_new = _rope(qk_ref[0, :, ATTN_W + g * GROUP_W:ATTN_W + (g + 1) * GROUP_W], cs, sn)
        v_new = v_refs[g][0]

        ko_ref[0, 0:n_old - t_new, :] = kb_ref[0, t_new:n_old, :]
        vo_ref[0, 0:n_old - t_new, :] = vb_ref[0, t_new:n_old, :]
        ko_ref[0, n_old - t_new:n_old, :] = k_new[0:t_new, :]
        vo_ref[0, n_old - t_new:n_old, :] = v_new[0:t_new, :]

        k_old, v_old = kb_ref[0].astype(BF16), vb_ref[0].astype(BF16)
        dist_old = (n_old + lax.broadcasted_iota(jnp.int32, (tp, n_old), 0)
                    - lax.broadcasted_iota(jnp.int32, (tp, n_old), 1))
        ok_old = jnp.logical_and(dist_old <= win, (dist_old & (dil - 1)) == 0)
        dist_new = (lax.broadcasted_iota(jnp.int32, (tp, tp), 0) - lax.broadcasted_iota(jnp.int32, (tp, tp), 1))
        ok_new = jnp.logical_and(jnp.logical_and(dist_new >= 0, dist_new <= win), (dist_new & (dil - 1)) == 0)
        os, ms, ds = [], [], []
        for h in range(GROUP_W // HEAD_DIM):
            hl = slice(h * HEAD_DIM, (h + 1) * HEAD_DIM)
            s_old = jnp.where(ok_old, _dot_nt(q[:, hl], k_old[:, hl]), -jnp.inf)
            s_new = jnp.where(ok_new, _dot_nt(q[:, hl], k_new[:, hl].astype(BF16)), -jnp.inf)
            m, (p_old, p_new), den = _softmax_parts([s_old, s_new])
            os.append(_dot(p_old.astype(BF16), v_old[:, hl]) + _dot(p_new.astype(BF16), v_new[:, hl].astype(BF16)))
            ms.append(m)
            ds.append(den)
        parts.append((jnp.concatenate(os, axis=1), _lanes_per_head(ms), _lanes_per_head(ds)))

    m_all = functools.reduce(jnp.maximum, [p[1] for p in parts])
    scl = [jnp.exp(p[1] - m_all) for p in parts]
    tot = functools.reduce(lambda a, b: a + b, [p[2] * s for p, s in zip(parts, scl)])
    y_ref[0] = jnp.concatenate([p[0] * (s / tot) for p, s in zip(parts, scl)], axis=1)


def _attn_decode(qk, vs, cos_t, sin_t, caches, t_new):
    bsz, tp, _ = qk.shape
    ng = len(ATTN_GROUPS)
    cache_specs = [pl.BlockSpec((1, c.shape[1], GROUP_W), lambda i: (i, 0, 0)) for c in caches]
    outs = pl.pallas_call(
        functools.partial(_attn_decode_body, t_new=t_new), grid=(bsz,),
        in_specs=[pl.BlockSpec((1, tp, 2 * ATTN_W), lambda i: (i, 0, 0))]
                 + [pl.BlockSpec((1, tp, GROUP_W), lambda i: (i, 0, 0))] * ng
                 + [_resident((tp, GROUP_W)), _resident((tp, GROUP_W))] + cache_specs,
        out_specs=[pl.BlockSpec((1, tp, ATTN_W), lambda i: (i, 0, 0))] + cache_specs,
        out_shape=[jax.ShapeDtypeStruct((bsz, tp, ATTN_W), F32)]
                  + [jax.ShapeDtypeStruct(c.shape, F32) for c in caches],
        compiler_params=_cparams("parallel"), name="attn_decode",
    )(qk, *vs, cos_t, sin_t, *caches)
    return outs[0], outs[1:]


def _rwkv_body(cc_ref, prev_ref, s0_ref, mix_ref, w0_ref, wup_ref, a0_ref, aup_ref, gup_ref,
               kk_ref, ka_ref, rk_ref, lng_ref, lnb_ref, y_ref, s_ref, prev_scr, *, t_valid):
    c = pl.program_id(1)
    chunk = cc_ref.shape[1]

    @pl.when(c == 0)
    def _():
        s_ref[...] = s0_ref[...]
        prev_scr[...] = prev_ref[0]

    f = cc_ref[0]
    row = lax.broadcasted_iota(jnp.int32, (chunk, 1), 0)
    shifted = jnp.where(row == 0, prev_scr[...], pltpu.roll(f, 1, 0))
    prev_scr[...] = f[chunk - 1:chunk, :]
    xs = f + (shifted - f) * mix_ref[...]

    r = xs[:, 0:RWKV_W]
    k = xs[:, RWKV_W:2 * RWKV_W]
    v = xs[:, 2 * RWKV_W:3 * RWKV_W]
    o1 = 3 * RWKV_W
    wlo = xs[:, o1:o1 + LORA_DECAY]
    alo = xs[:, o1 + LORA_DECAY:o1 + LORA_DECAY + LORA_AAA]
    glo = xs[:, o1 + LORA_DECAY + LORA_AAA:]
    w = -jax.nn.softplus(-(w0_ref[...] + _dot(jnp.tanh(wlo).astype(BF16), wup_ref[...]))) - 0.5
    logw = -jnp.exp(w)
    a = jax.nn.sigmoid(a0_ref[...] + _dot(alo.astype(BF16), aup_ref[...]))
    gate = _dot(jax.nn.sigmoid(glo).astype(BF16), gup_ref[...])
    kk_all = k * kk_ref[...]
    k = k * (1.0 + (a - 1.0) * ka_ref[...])

    masked = t_valid is not None
    if masked:
        valid = (c * chunk + row) < t_valid
        logw = jnp.where(valid, logw, 0.0)

    ri = lax.broadcasted_iota(jnp.int32, (chunk, chunk), 0)
    ci = lax.broadcasted_iota(jnp.int32, (chunk, chunk), 1)
    incl, strict = ri >= ci, ri > ci
    eye = (ri == ci).astype(F32)
    cum = jnp.dot(incl.astype(F32), logw, preferred_element_type=F32, precision=lax.Precision.HIGHEST)
    dec_incl, dec_excl, dec_inv = jnp.exp(cum), jnp.exp(cum - logw), jnp.exp(-cum)
    dec_all = dec_incl[chunk - 1:chunk, :]

    n_double = max((chunk - 1).bit_length() - 1, 0)
    ys = []
    for h in range(RWKV_HEADS):
        hl = slice(h * HEAD_DIM, (h + 1) * HEAD_DIM)
        kk = kk_all[:, hl]
        kk = kk / jnp.maximum(jnp.sqrt(jnp.sum(kk * kk, axis=-1, keepdims=True)), 1e-12)
        a_vec, b_vec, k_h = -kk, kk * a[:, hl], k[:, hl]
        if masked:
            a_vec = jnp.where(valid, a_vec, 0.0)
            b_vec = jnp.where(valid, b_vec, 0.0)
            k_h = jnp.where(valid, k_h, 0.0)
        r_h, v_h = r[:, hl], v[:, hl]
        a_t = (a_vec * dec_excl[:, hl]).astype(BF16)
        r_t = (r_h * dec_incl[:, hl]).astype(BF16)
        b_t = (b_vec * dec_inv[:, hl]).astype(BF16)
        k_t = (k_h * dec_inv[:, hl]).astype(BF16)
        v_b = v_h.astype(BF16)
        s0 = s_ref[0, h]
        s0_b = s0.astype(BF16)

        a_ab = jnp.where(strict, _dot_nt(a_t, b_t), 0.0)
        a_ak = jnp.where(strict, _dot_nt(a_t, k_t), 0.0)
        a_rb = jnp.where(incl, _dot_nt(r_t, b_t), 0.0)
        a_rk = jnp.where(incl, _dot_nt(r_t, k_t), 0.0)
        rhs = _dot_nt(a_t, s0_b) + _dot(a_ak.astype(BF16), v_b)
        pw = a_ab
        inv = eye + pw
        for _ in range(n_double):
            pw_b = pw.astype(BF16)
            pw = _dot(pw_b, pw_b)
            inv = inv + _dot(inv.astype(BF16), pw.astype(BF16))
        u = _dot(inv.astype(BF16), rhs.astype(BF16))
        u_b = u.astype(BF16)
        o = _dot_nt(r_t, s0_b) + _dot(a_rb.astype(BF16), u_b) + _dot(a_rk.astype(BF16), v_b)
        s_new = s0 + _dot(u.T.astype(BF16), b_t) + _dot(v_h.T.astype(BF16), k_t)
        s_ref[0, h] = s_new * dec_all[:, hl]

        mu = jnp.mean(o, axis=-1, keepdims=True)
        var = jnp.mean(jnp.square(o - mu), axis=-1, keepdims=True)
        on = (o - mu) * lax.rsqrt(var + LNX_EPS) * lng_ref[:, hl] + lnb_ref[:, hl]
        bonus = jnp.sum(r_h * k[:, hl] * rk_ref[:, hl], axis=-1, keepdims=True) * v_h
        ys.append((on + bonus) * gate[:, hl])
    y_ref[0] = jnp.concatenate(ys, axis=1)


def _rwkv(cols_c, prev, s0, p, t_valid=None):
    bsz, t, _ = cols_c.shape
    chunk = _row_tile(t, RWKV_CHUNK)
    vec = lambda a: a.reshape(1, -1)
    small = [vec(p['shift_mix']), vec(p['w0']), p['w_up'], vec(p['a0']), p['a_up'], p['g_up'],
             vec(p['k_k']), vec(p['k_a']), vec(p['r_k']), vec(p['lnx_g']), vec(p['lnx_b'])]
    state = pl.BlockSpec((1, RWKV_HEADS, HEAD_DIM, HEAD_DIM), lambda b, c: (b, 0, 0, 0))
    return pl.pallas_call(
        functools.partial(_rwkv_body, t_valid=t_valid), grid=(bsz, t // chunk),
        in_specs=[pl.BlockSpec((1, chunk, C_COLS), lambda b, c: (b, c, 0)),
                  pl.BlockSpec((1, 1, C_COLS), lambda b, c: (b, 0, 0)), state]
                 + [_resident(a.shape) for a in small],
        out_specs=[pl.BlockSpec((1, chunk, RWKV_W), lambda b, c: (b, c, 0)), state],
        out_shape=[jax.ShapeDtypeStruct((bsz, t, RWKV_W), F32),
                   jax.ShapeDtypeStruct(s0.shape, F32)],
        scratch_shapes=[pltpu.VMEM((1, C_COLS), F32)],
        compiler_params=_cparams("parallel", "arbitrary"), name="rwkv",
    )(cols_c, prev, s0, *small)


def _xattn_body(x_ref, g_ref, wq_ref, wo_ref, mk_ref, mv_ref, o_ref):
    x = x_ref[0]
    h = _rms(x, g_ref[...]).astype(BF16)
    q = _dot(h, wq_ref[...]) * (XATTN_HEAD_DIM ** -0.5)
    outs = []
    for hh in range(XATTN_HEADS):
        hl = slice(hh * XATTN_HEAD_DIM, (hh + 1) * XATTN_HEAD_DIM)
        s = _dot_nt(q[:, hl].astype(BF16), mk_ref[0, :, hl].astype(BF16))
        m = s.max(axis=-1, keepdims=True)
        p = jnp.exp(s - m)
        den = p.sum(axis=-1, keepdims=True)
        outs.append(_dot(p.astype(BF16), mv_ref[0, :, hl].astype(BF16)) / den)
    o = jnp.concatenate(outs, axis=1).astype(BF16)
    o_ref[0] = x + _dot(o, wo_ref[...])


def _xattn(x, g, wq, wo, mk, mv):
    bsz, t, d = x.shape
    tq = _row_tile(t, 512)
    nm = mk.shape[1]
    row = pl.BlockSpec((1, tq, d), lambda b, i: (b, i, 0))
    mem = pl.BlockSpec((1, nm, d), lambda b, i: (b, 0, 0))
    return pl.pallas_call(
        _xattn_body, grid=(bsz, t // tq),
        in_specs=[row, _resident((1, d)), _resident(wq.shape), _resident(wo.shape), mem, mem],
        out_specs=row, out_shape=jax.ShapeDtypeStruct(x.shape, F32),
        compiler_params=_cparams("parallel", "parallel"), name="xattn",
    )(x, g.reshape(1, d), wq, wo, mk, mv)


def _final_norm_body(x_ref, g_ref, o_ref):
    o_ref[...] = _rms(x_ref[...], g_ref[...])


def _final_norm(x, g):
    m, d = x.shape
    tm = _row_tile(m, 1024)
    row = pl.BlockSpec((tm, d), lambda i: (i, 0))
    return pl.pallas_call(
        _final_norm_body, grid=(m // tm,), in_specs=[row, _resident((1, d))], out_specs=row,
        out_shape=jax.ShapeDtypeStruct((m, d), F32), compiler_params=_cparams("parallel"),
        name="final_norm",
    )(x, g.reshape(1, d))


def _pad_rows(a, rows):
    return jnp.pad(a, ((0, 0), (0, rows - a.shape[1]), (0, 0)))


def _trunk(x, pos, P, mem, mem_k_cache, mem_v_cache, conv_cache, win_caches, shift_cache, wkv_cache):
    prompt = mem is not None
    bsz, t, d = x.shape
    ng = len(ATTN_GROUPS)
    tp = t if prompt else -(-t // SUBLANES) * SUBLANES
    cos_t, sin_t = _rope_tables(pos if prompt else pos[0] + jnp.arange(tp, dtype=pos.dtype))
    x = x.reshape(bsz * t, d)
    conv_new, shift_new, wkv_new, memk_new, memv_new = [], [], [], [], []
    wink_new = [[] for _ in range(ng)]
    winv_new = [[] for _ in range(ng)]
    for l in range(DEPTH):
        x = _ffn(x, P['ffn1_norm'][l], P['ffn1_w_gate'][l], P['ffn1_w_up'][l], P['ffn1_w_down'][l])

        cols = _norm_proj(x, P['mix_norm'][l], P['w_in'][l],
                          (A_COLS, 2 * ATTN_W) + (GROUP_W,) * ng + (C_COLS,))
        cols = [c.reshape(bsz, t, c.shape[1]) for c in cols]
        cols_a, qk, vs, cols_c = cols[0], cols[1], cols[2:2 + ng], cols[-1]

        hist = jnp.zeros((bsz, CONV_HIST, CONV_CH), F32) if prompt else conv_cache[l]
        y_a, conv_state = _conv_mixer(cols_a, hist, P['conv_w'][l], P['conv_b'][l],
                                      P['conv_ln_g'][l], P['conv_ln_b'][l])
        conv_new.append(conv_state)

        if prompt:
            y_bs, k_rot = _attn_prompt(qk, vs, cos_t, sin_t)
            y_bs = [y.reshape(bsz * t, GROUP_W) for y in y_bs]
            for g, (win, dil) in enumerate(ATTN_GROUPS):
                keep = min(win, t)
                wink_new[g].append(k_rot[g][:, t - keep:].reshape(bsz, keep, GROUP_W // HEAD_DIM, HEAD_DIM))
                winv_new[g].append(vs[g][:, t - keep:].reshape(bsz, keep, GROUP_W // HEAD_DIM, HEAD_DIM))
        else:
            caches = []
            for g in range(ng):
                for buf in win_caches[g]:
                    caches.append(buf[l].reshape(bsz, buf.shape[2], GROUP_W))
            y_b, new_caches = _attn_decode(_pad_rows(qk, tp), [_pad_rows(v, tp) for v in vs],
                                           cos_t, sin_t, caches, t)
            y_bs = [y_b[:, :t].reshape(bsz * t, ATTN_W)]
            for g in range(ng):
                shape = win_caches[g][0].shape[1:]
                wink_new[g].append(new_caches[2 * g].reshape(shape))
                winv_new[g].append(new_caches[2 * g + 1].reshape(shape))

        rp = {k[5:]: v[l] for k, v in P.items() if k.startswith('rwkv_')}
        if prompt:
            prev = jnp.zeros((bsz, 1, C_COLS), F32)
            s0 = jnp.zeros((bsz, RWKV_HEADS, HEAD_DIM, HEAD_DIM), F32)
            y_c, s_new = _rwkv(cols_c, prev, s0, rp)
        else:
            y_c, s_new = _rwkv(_pad_rows(cols_c, tp), shift_cache[l][:, None, :], wkv_cache[l], rp,
                               t_valid=None if tp == t else t)
            y_c = y_c[:, :t]
        shift_new.append(cols_c[:, -1])
        wkv_new.append(s_new)

        x = _out_proj(x, [y_a.reshape(bsz * t, CONV_CH)] + y_bs + [y_c.reshape(bsz * t, RWKV_W)],
                      P['w_out'][l])

        if prompt:
            nm = mem.shape[1]
            mk, mv = _norm_proj(mem.reshape(bsz * nm, d), P['mem_norm'][l], P['xattn_w_kv'][l], (d, d))
            mk, mv = mk.reshape(bsz, nm, d), mv.reshape(bsz, nm, d)
            memk_new.append(mk.reshape(bsz, nm, XATTN_HEADS, XATTN_HEAD_DIM))
            memv_new.append(mv.reshape(bsz, nm, XATTN_HEADS, XATTN_HEAD_DIM))
        else:
            nm = mem_k_cache.shape[2]
            mk = mem_k_cache[l].reshape(bsz, nm, d)
            mv = mem_v_cache[l].reshape(bsz, nm, d)
        x = _xattn(x.reshape(bsz, t, d), P['xattn_norm'][l], P['xattn_w_q'][l], P['xattn_w_o'][l],
                   mk, mv).reshape(bsz * t, d)

        x = _ffn(x, P['ffn2_norm'][l], P['ffn2_w_gate'][l], P['ffn2_w_up'][l], P['ffn2_w_down'][l])

    y = _final_norm(x, P['final_norm']).reshape(bsz, t, d)
    stack = jnp.stack
    outs = [y, stack(conv_new)]
    for g in range(ng):
        outs += [stack(wink_new[g]), stack(winv_new[g])]
    outs += [stack(shift_new), stack(wkv_new)]
    if prompt:
        outs += [stack(memk_new), stack(memv_new)]
    return outs


_MATMUL_WEIGHTS = ('ffn1_w_gate', 'ffn1_w_up', 'ffn1_w_down', 'w_in', 'w_out', 'rwkv_w_up', 'rwkv_a_up',
                   'rwkv_g_up', 'xattn_w_q', 'xattn_w_kv', 'xattn_w_o', 'ffn2_w_gate', 'ffn2_w_up',
                   'ffn2_w_down')


def kernel(x_prompt, x_sample, cache_conv, cache_win1_k, cache_win1_v, cache_win2_k, cache_win2_v,
           cache_win3_k, cache_win3_v, state_shift, state_wkv, cache_mem_k, cache_mem_v, mem_prompt,
           ffn1_norm, ffn1_w_gate, ffn1_w_up, ffn1_w_down, mix_norm, w_in, w_out,
           conv_w, conv_b, conv_ln_g, conv_ln_b,
           rwkv_shift_mix, rwkv_w0, rwkv_w_up, rwkv_a0, rwkv_a_up, rwkv_g_up, rwkv_k_k, rwkv_k_a,
           rwkv_r_k, rwkv_lnx_g, rwkv_lnx_b,
           xattn_norm, mem_norm, xattn_w_q, xattn_w_kv, xattn_w_o,
           ffn2_norm, ffn2_w_gate, ffn2_w_up, ffn2_w_down, final_norm):
    P = dict(ffn1_norm=ffn1_norm, ffn1_w_gate=ffn1_w_gate, ffn1_w_up=ffn1_w_up, ffn1_w_down=ffn1_w_down,
             mix_norm=mix_norm, w_in=w_in, w_out=w_out,
             conv_w=conv_w, conv_b=conv_b, conv_ln_g=conv_ln_g, conv_ln_b=conv_ln_b,
             rwkv_shift_mix=rwkv_shift_mix, rwkv_w0=rwkv_w0, rwkv_w_up=rwkv_w_up, rwkv_a0=rwkv_a0,
             rwkv_a_up=rwkv_a_up, rwkv_g_up=rwkv_g_up, rwkv_k_k=rwkv_k_k, rwkv_k_a=rwkv_k_a,
             rwkv_r_k=rwkv_r_k, rwkv_lnx_g=rwkv_lnx_g, rwkv_lnx_b=rwkv_lnx_b,
             xattn_norm=xattn_norm, mem_norm=mem_norm, xattn_w_q=xattn_w_q, xattn_w_kv=xattn_w_kv,
             xattn_w_o=xattn_w_o, ffn2_norm=ffn2_norm, ffn2_w_gate=ffn2_w_gate, ffn2_w_up=ffn2_w_up,
             ffn2_w_down=ffn2_w_down, final_norm=final_norm)
    for name in _MATMUL_WEIGHTS:
        P[name] = P[name].astype(BF16)

    pos_p = jnp.arange(x_prompt.shape[1], dtype=jnp.int32)
    outs_p = _trunk(x_prompt, pos_p, P, mem_prompt, None, None, None, None, None, None)

    pos_s = PAST_LEN + jnp.arange(x_sample.shape[1], dtype=jnp.int32)
    win_caches = [(cache_win1_k, cache_win1_v), (cache_win2_k, cache_win2_v), (cache_win3_k, cache_win3_v)]
    outs_s = _trunk(x_sample, pos_s, P, None, cache_mem_k, cache_mem_v, cache_conv, win_caches,
                    state_shift, state_wkv)

    return (outs_p[0], outs_s[0], *outs_p[1:], *outs_s[1:])
```

```python
import functools

import jax
import jax.numpy as jnp
from jax import lax
from jax.experimental import pallas as pl
from jax.experimental.pallas import tpu as pltpu

F32 = jnp.float32
BF16 = jnp.bfloat16

D_MODEL = 1024
DEPTH = 4
HEAD_DIM = 64
CONV_CH = 256
CONV_WIDTH = 31
ATTN_GROUPS = ((128, 1), (512, 4), (2048, 16))
N_GROUPS = len(ATTN_GROUPS)
GROUP_W = 2 * HEAD_DIM
ATTN_W = GROUP_W * N_GROUPS
ATTN_SCALE = HEAD_DIM ** -0.5
ATTN_BLOCK = 128
RWKV_HEADS = 6
RWKV_W = RWKV_HEADS * HEAD_DIM
LORA_DECAY = 64
LORA_AAA = 64
LORA_GATE = 128
LNX_EPS = 64e-5
ROPE_THETA = 10000.0
XATTN_HEADS = 4
XATTN_HEAD_DIM = D_MODEL // XATTN_HEADS
NORM_EPS = 1e-6
A_COLS = 2 * CONV_CH
B_COLS = 3 * ATTN_W
C_COLS = 3 * RWKV_W + LORA_DECAY + LORA_AAA + LORA_GATE
PAST_LEN = 8192

VMEM_LIMIT_BYTES = 56 * 1024 * 1024
LANES = 128
SUBLANES = 8
ROW_TILE = 512
FFN_TF = 256
RWKV_CHUNK = 64
RWKV_BATCH = 4


def _cparams(*sem):
    return pltpu.CompilerParams(dimension_semantics=sem, vmem_limit_bytes=VMEM_LIMIT_BYTES)


def _resident(shape):
    nd = len(shape)
    return pl.BlockSpec(shape, lambda *_: (0,) * nd, pipeline_mode=pl.Buffered(1))


def _layer(arr, l):
    nd = arr.ndim
    return pl.BlockSpec((None,) + arr.shape[1:], lambda *_: (l,) + (0,) * (nd - 1),
                        pipeline_mode=pl.Buffered(1))


def _carried(carry, shape, n_in):
    if carry is None:
        return [], [], {}
    assert carry.shape == shape
    return [carry], [pl.BlockSpec(memory_space=pl.ANY)], {n_in: None}


def _rms(x, g):
    return x * lax.rsqrt(jnp.mean(x * x, axis=-1, keepdims=True) + NORM_EPS) * g


def _dot(a, b):
    return jnp.dot(a, b, preferred_element_type=F32)


def _dot_nt(a, b):
    return lax.dot_general(a, b, (((1,), (1,)), ((), ())), preferred_element_type=F32)


def _row_tile(m, target):
    t = min(m, target)
    assert m % t == 0, (m, t)
    return t


def _ffn_body(x_ref, g_ref, wg_ref, wu_ref, wd_ref, o_ref):
    x = x_ref[...]
    h = _rms(x, g_ref[...]).astype(BF16)
    acc = jnp.zeros(x.shape, F32)
    for j in range(wg_ref.shape[1] // FFN_TF):
        sl = slice(j * FFN_TF, (j + 1) * FFN_TF)
        gate = _dot(h, wg_ref[:, sl])
        up = _dot(h, wu_ref[:, sl])
        act = (gate * jax.nn.sigmoid(gate) * up).astype(BF16)
        acc = acc + _dot(act, wd_ref[sl, :])
    o_ref[...] = x + 0.5 * acc


def _ffn(l, x, g, wg, wu, wd):
    m, d = x.shape
    tm = _row_tile(m, ROW_TILE)
    row = pl.BlockSpec((tm, d), lambda i: (i, 0))
    return pl.pallas_call(
        _ffn_body, grid=(m // tm,),
        in_specs=[row, _layer(g, l), _layer(wg, l), _layer(wu, l), _layer(wd, l)],
        out_specs=row, out_shape=jax.ShapeDtypeStruct((m, d), F32),
        compiler_params=_cparams("parallel"), name="ffn",
    )(x, g, wg, wu, wd)


def _norm_proj_body(x_ref, g_ref, w_ref, *o_refs):
    h = _rms(x_ref[...], g_ref[...]).astype(BF16)
    off = 0
    for o_ref in o_refs:
        n = o_ref.shape[1]
        o_ref[...] = _dot(h, w_ref[:, off:off + n])
        off += n


def _norm_proj(l, x, g, w, splits):
    m, d = x.shape
    tm = _row_tile(m, ROW_TILE)
    return pl.pallas_call(
        _norm_proj_body, grid=(m // tm,),
        in_specs=[pl.BlockSpec((tm, d), lambda i: (i, 0)), _layer(g, l), _layer(w, l)],
        out_specs=[pl.BlockSpec((tm, n), lambda i: (i, 0)) for n in splits],
        out_shape=[jax.ShapeDtypeStruct((m, n), F32) for n in splits],
        compiler_params=_cparams("parallel"), name="norm_proj",
    )(x, g, w)


def _out_proj_body(x_ref, *refs):
    y_refs, w_ref, o_ref = refs[:-2], refs[-2], refs[-1]
    acc = x_ref[...]
    off = 0
    for y_ref in y_refs:
        n = y_ref.shape[1]
        acc = acc + _dot(y_ref[...].astype(BF16), w_ref[off:off + n, :])
        off += n
    o_ref[...] = acc


def _out_proj(l, x, ys, w):
    m, d = x.shape
    tm = _row_tile(m, ROW_TILE)
    row = lambda n: pl.BlockSpec((tm, n), lambda i: (i, 0))
    return pl.pallas_call(
        _out_proj_body, grid=(m // tm,),
        in_specs=[row(d)] + [row(y.shape[1]) for y in ys] + [_layer(w, l)],
        out_specs=row(d), out_shape=jax.ShapeDtypeStruct((m, d), F32),
        compiler_params=_cparams("parallel"), name="out_proj",
    )(x, *ys, w)


CONV_HIST = CONV_WIDTH - 1
CONV_PAD = 32


def _conv_body(ca_ref, hist_ref, w_ref, b_ref, lg_ref, lb_ref, y_ref, st_ref, u_scr, *, chunk):
    t = ca_ref.shape[1]
    x = ca_ref[0]
    u_scr[CONV_PAD - CONV_HIST:CONV_PAD, :] = hist_ref[0]
    u_scr[CONV_PAD:CONV_PAD + t, :] = x[:, :CONV_CH] * jax.nn.sigmoid(x[:, CONV_CH:])
    st_ref[0] = u_scr[CONV_PAD + t - CONV_HIST:CONV_PAD + t, :]
    base = CONV_PAD - CONV_HIST
    for c in range(t // chunk):
        acc = jnp.zeros((chunk, CONV_CH), F32) + b_ref[...]
        for j in range(CONV_WIDTH):
            r0 = c * chunk + base + j
            acc = acc + u_scr[r0:r0 + chunk, :] * w_ref[j:j + 1, :]
        mu = jnp.mean(acc, axis=-1, keepdims=True)
        var = jnp.mean(jnp.square(acc - mu), axis=-1, keepdims=True)
        yn = (acc - mu) * lax.rsqrt(var + 1e-5) * lg_ref[...] + lb_ref[...]
        y_ref[0, c * chunk:(c + 1) * chunk, :] = yn * jax.nn.sigmoid(yn)


def _conv_mixer(l, cols_a, hist, w, b, lg, lb):
    bsz, t, _ = cols_a.shape
    chunk = _row_tile(t, 128)
    return pl.pallas_call(
        functools.partial(_conv_body, chunk=chunk), grid=(bsz,),
        in_specs=[pl.BlockSpec((1, t, A_COLS), lambda i: (i, 0, 0)),
                  pl.BlockSpec((1, CONV_HIST, CONV_CH), lambda i: (i, 0, 0)),
                  _layer(w, l), _layer(b, l), _layer(lg, l), _layer(lb, l)],
        out_specs=[pl.BlockSpec((1, t, CONV_CH), lambda i: (i, 0, 0)),
                   pl.BlockSpec((1, CONV_HIST, CONV_CH), lambda i: (i, 0, 0))],
        out_shape=[jax.ShapeDtypeStruct((bsz, t, CONV_CH), F32),
                   jax.ShapeDtypeStruct((bsz, CONV_HIST, CONV_CH), F32)],
        scratch_shapes=[pltpu.VMEM((CONV_PAD + t, CONV_CH), F32)],
        compiler_params=_cparams("parallel"), name="conv_mixer",
    )(cols_a, hist, w, b, lg, lb)


def _rope_tables(pos):
    half = HEAD_DIM // 2
    inv = jnp.power(ROPE_THETA, -jnp.arange(half, dtype=F32) / half)
    ang = pos.astype(F32)[:, None] * inv[None, :]
    cos, sin = jnp.cos(ang), jnp.sin(ang)
    cos_t = jnp.concatenate([cos, cos], axis=-1)
    sin_t = jnp.concatenate([-sin, sin], axis=-1)
    reps = GROUP_W // HEAD_DIM
    return jnp.tile(cos_t, (1, reps)), jnp.tile(sin_t, (1, reps))


def _rope(x, cos, sin):
    half = HEAD_DIM // 2
    lane = lax.broadcasted_iota(jnp.int32, x.shape, 1)
    first_half = (lane & (HEAD_DIM - 1)) < half
    partner = jnp.where(first_half, pltpu.roll(x, GROUP_W - half, 1), pltpu.roll(x, half, 1))
    return x * cos + partner * sin


def _softmax_parts(scores):
    m = scores[0].max(axis=-1, keepdims=True)
    for s in scores[1:]:
        m = jnp.maximum(m, s.max(axis=-1, keepdims=True))
    ps = [jnp.exp(s - m) for s in scores]
    den = ps[0].sum(axis=-1, keepdims=True)
    for p in ps[1:]:
        den = den + p.sum(axis=-1, keepdims=True)
    return m, ps, den


def _lanes_per_head(vals):
    return jnp.concatenate([jnp.broadcast_to(v, (v.shape[0], HEAD_DIM)) for v in vals], axis=1)


def _mix_dilations(parts):
    m_all = functools.reduce(jnp.maximum, [p[1] for p in parts])
    scl = [jnp.exp(p[1] - m_all) for p in parts]
    tot = functools.reduce(lambda a, b: a + b, [p[2] * s for p, s in zip(parts, scl)])
    return [p[0] * (s / tot) for p, s in zip(parts, scl)]


def _attn_prompt_body(*refs):
    ng = N_GROUPS
    qk_ref, v_refs, cos_ref, sin_ref = refs[0], refs[1:1 + ng], refs[1 + ng], refs[2 + ng]
    rest = refs[3 + ng:]
    n_carry = len(rest) - (3 * ng + 9)
    rest = rest[n_carry:]
    y_refs, wk_refs, wv_refs = rest[:ng], rest[ng:2 * ng], rest[2 * ng:3 * ng]
    q_scr, kr_scr, pq_scr, pk_scr, pv_scr, po_scr, pm_scr, pd_scr, md_scr = rest[3 * ng:]
    t = qk_ref.shape[1]
    rc = min(t, 256)
    nblk = t // ATTN_BLOCK

    def rope_chunk(i, carry):
        rows = pl.ds(pl.multiple_of(i * rc, rc), rc)
        cs, sn = cos_ref[rows, :], sin_ref[rows, :]
        for g in range(ng):
            q_scr[g, rows, :] = _rope(qk_ref[0, rows, g * GROUP_W:(g + 1) * GROUP_W], cs, sn) * ATTN_SCALE
            kr_scr[g, rows, :] = _rope(qk_ref[0, rows, ATTN_W + g * GROUP_W:ATTN_W + (g + 1) * GROUP_W], cs, sn)
        return carry

    lax.fori_loop(0, t // rc, rope_chunk, 0)

    qi = lax.broadcasted_iota(jnp.int32, (ATTN_BLOCK, ATTN_BLOCK), 0)
    kj = lax.broadcasted_iota(jnp.int32, (ATTN_BLOCK, ATTN_BLOCK), 1)

    for g, (win, dil) in enumerate(ATTN_GROUPS):
        assert win // dil == ATTN_BLOCK and t % (dil * ATTN_BLOCK) == 0
        sub_len = t // dil
        sub_blocks = sub_len // ATTN_BLOCK

        keep = wk_refs[g].shape[-1]
        for j in range(keep // LANES):
            src = slice(t - keep + j * LANES, t - keep + (j + 1) * LANES)
            dst = slice(j * LANES, (j + 1) * LANES)
            wk_refs[g][:, dst] = kr_scr[g, src, :].T
            wv_refs[g][:, dst] = v_refs[g][0, src, :].T

        for r in range(dil):
            rows = pl.ds(r, sub_len, stride=dil) if dil > 1 else slice(0, t)
            dst = slice(r * sub_len, (r + 1) * sub_len)
            pq_scr[dst, :] = q_scr[g, rows, :].astype(BF16)
            pk_scr[dst, :] = kr_scr[g, rows, :].astype(BF16)
            pv_scr[dst, :] = v_refs[g][0, rows, :].astype(BF16)

        def block(i, carry):
            own = pl.ds(pl.multiple_of(i * ATTN_BLOCK, ATTN_BLOCK), ATTN_BLOCK)
            q = pq_scr[own, :]
            k_own, v_own = pk_scr[own, :], pv_scr[own, :]
            masks = [kj <= qi]
            if sub_blocks > 1:
                prev = pl.ds(pl.multiple_of(jnp.maximum(i - 1, 0) * ATTN_BLOCK, ATTN_BLOCK), ATTN_BLOCK)
                k_prev, v_prev = pk_scr[prev, :], pv_scr[prev, :]
                has_prev = lax.rem(i, sub_blocks) > 0
                masks.append(jnp.logical_and(kj >= qi, has_prev))
            os, ms, ds = [], [], []
            for h in range(GROUP_W // HEAD_DIM):
                hl = slice(h * HEAD_DIM, (h + 1) * HEAD_DIM)
                ks, vs = [k_own[:, hl]], [v_own[:, hl]]
                if sub_blocks > 1:
                    ks.append(k_prev[:, hl])
                    vs.append(v_prev[:, hl])
                scores = [jnp.where(mk, _dot_nt(q[:, hl], kk), -jnp.inf) for mk, kk in zip(masks, ks)]
                m, ps, den = _softmax_parts(scores)
                o = _dot(ps[0].astype(BF16), vs[0])
                for p, vv in zip(ps[1:], vs[1:]):
                    o = o + _dot(p.astype(BF16), vv)
                os.append(o)
                ms.append(m)
                ds.append(den)
            po_scr[own, :] = jnp.concatenate(os, axis=1)
            pm_scr[own, :] = _lanes_per_head(ms)
            pd_scr[own, :] = _lanes_per_head(ds)
            return carry

        lax.fori_loop(0, nblk, block, 0)

        for r in range(dil):
            rows = pl.ds(r, sub_len, stride=dil) if dil > 1 else slice(0, t)
            src = slice(r * sub_len, (r + 1) * sub_len)
            y_refs[g][0, rows, :] = po_scr[src, :]
            md_scr[g, rows, :] = pm_scr[src, :]
            md_scr[ng + g, rows, :] = pd_scr[src, :]

    def combine_chunk(i, carry):
        rows = pl.ds(pl.multiple_of(i * rc, rc), rc)
        mixed = _mix_dilations([(y_refs[g][0, rows, :], md_scr[g, rows, :], md_scr[ng + g, rows, :])
                                for g in range(ng)])
        for g in range(ng):
            y_refs[g][0, rows, :] = mixed[g]
        return carry

    lax.fori_loop(0, t // rc, combine_chunk, 0)


def _attn_prompt(l, qk, vs, cos_t, sin_t, carries):
    bsz, t, _ = qk.shape
    ng = N_GROUPS
    grp = pl.BlockSpec((1, t, GROUP_W), lambda i: (i, 0, 0))
    grp_in = pl.BlockSpec((1, t, GROUP_W), lambda i: (i, 0, 0), pipeline_mode=pl.Buffered(1))
    keeps = [min(win, t) for win, _ in ATTN_GROUPS]
    win_shapes = [(DEPTH, bsz, GROUP_W, kp) for kp in keeps] * 2
    win_specs = [pl.BlockSpec((None, None, GROUP_W, kp), lambda i: (l, i, 0, 0)) for kp in keeps] * 2
    n_in = 3 + ng
    c_in, c_specs, aliases = [], [], {}
    for j, shp in enumerate(win_shapes):
        a, s, al = _carried(None if carries is None else carries[j], shp, n_in + j)
        c_in, c_specs = c_in + a, c_specs + s
        aliases.update({k: ng + j for k in al})
    outs = pl.pallas_call(
        _attn_prompt_body, grid=(bsz,),
        in_specs=[pl.BlockSpec((1, t, 2 * ATTN_W), lambda i: (i, 0, 0), pipeline_mode=pl.Buffered(1))]
                 + [grp_in] * ng + [_resident((t, GROUP_W)), _resident((t, GROUP_W))] + c_specs,
        out_specs=[grp] * ng + win_specs,
        out_shape=[jax.ShapeDtypeStruct((bsz, t, GROUP_W), F32)] * ng
                  + [jax.ShapeDtypeStruct(s, F32) for s in win_shapes],
        scratch_shapes=[pltpu.VMEM((ng, t, GROUP_W), F32), pltpu.VMEM((ng, t, GROUP_W), F32),
                        pltpu.VMEM((t, GROUP_W), BF16), pltpu.VMEM((t, GROUP_W), BF16),
                        pltpu.VMEM((t, GROUP_W), BF16),
                        pltpu.VMEM((t, GROUP_W), F32), pltpu.VMEM((t, GROUP_W), F32),
                        pltpu.VMEM((t, GROUP_W), F32),
                        pltpu.VMEM((2 * ng, t, GROUP_W), F32)],
        input_output_aliases=aliases,
        compiler_params=_cparams("parallel"), name="attn_prompt",
    )(qk, *vs, cos_t, sin_t, *c_in)
    return outs[:ng], outs[ng:]


def _attn_decode_body(*refs, t_new):
    ng = N_GROUPS
    qk_ref, v_refs, cos_ref, sin_ref = refs[0], refs[1:1 + ng], refs[1 + ng], refs[2 + ng]
    cache_refs = refs[3 + ng:3 + 3 * ng]
    rest = refs[3 + 3 * ng:]
    n_carry = len(rest) - (1 + 2 * ng + 1)
    rest = rest[n_carry:]
    y_ref, out_refs, pad_scr = rest[0], rest[1:1 + 2 * ng], rest[-1]
    tp = qk_ref.shape[1]
    cs, sn = cos_ref[...], sin_ref[...]
    tail_lane = lax.broadcasted_iota(jnp.int32, (GROUP_W, LANES), 1)
    parts = []
    for g, (win, dil) in enumerate(ATTN_GROUPS):
        assert dil & (dil - 1) == 0
        n_old = cache_refs[g].shape[-1]
        q = (_rope(qk_ref[0, :, g * GROUP_W:(g + 1) * GROUP_W], cs, sn) * ATTN_SCALE).astype(BF16)
        k_new = _rope(qk_ref[0, :, ATTN_W + g * GROUP_W:ATTN_W + (g + 1) * GROUP_W], cs, sn)
        v_new = v_refs[g][0]
        old = []
        for new_rows, c_ref, o_ref in ((k_new, cache_refs[g], out_refs[g]),
                                       (v_new, cache_refs[ng + g], out_refs[ng + g])):
            c_old = c_ref[...]
            moved = pltpu.roll(c_old, n_old - t_new, 1)
            pad_scr[...] = jnp.zeros(pad_scr.shape, F32)
            pad_scr[LANES - t_new:LANES, :] = new_rows[0:t_new, :]
            tail = jnp.where(tail_lane >= LANES - t_new, pad_scr[...].T, moved[:, n_old - LANES:])
            if n_old > LANES:
                o_ref[:, 0:n_old - LANES] = moved[:, 0:n_old - LANES]
            o_ref[:, n_old - LANES:] = tail
            old.append(c_old.astype(BF16))
        k_old, v_old = old
        dist_old = (n_old + lax.broadcasted_iota(jnp.int32, (tp, n_old), 0)
                    - lax.broadcasted_iota(jnp.int32, (tp, n_old), 1))
        ok_old = jnp.logical_and(dist_old <= win, (dist_old & (dil - 1)) == 0)
        dist_new = (lax.broadcasted_iota(jnp.int32, (tp, tp), 0) - lax.broadcasted_iota(jnp.int32, (tp, tp), 1))
        ok_new = jnp.logical_and(jnp.logical_and(dist_new >= 0, dist_new <= win), (dist_new & (dil - 1)) == 0)
        os, ms, ds = [], [], []
        for h in range(GROUP_W // HEAD_DIM):
            hl = slice(h * HEAD_DIM, (h + 1) * HEAD_DIM)
            s_old = jnp.where(ok_old, _dot(q[:, hl], k_old[hl, :]), -jnp.inf)
            s_new = jnp.where(ok_new, _dot_nt(q[:, hl], k_new[:, hl].astype(BF16)), -jnp.inf)
            m, (p_old, p_new), den = _softmax_parts([s_old, s_new])
            os.append(_dot_nt(p_old.astype(BF16), v_old[hl, :]) + _dot(p_new.astype(BF16), v_new[:, hl].astype(BF16)))
            ms.append(m)
            ds.append(den)
        parts.append((jnp.concatenate(os, axis=1), _lanes_per_head(ms), _lanes_per_head(ds)))
    y_ref[0] = jnp.concatenate(_mix_dilations(parts), axis=1)


def _attn_decode(l, qk, vs, cos_t, sin_t, caches, carries, t_new):
    bsz, tp, _ = qk.shape
    ng = N_GROUPS
    cache_specs = [pl.BlockSpec((None, None, GROUP_W, c.shape[-1]), lambda i: (l, i, 0, 0)) for c in caches]
    n_in = 3 + 3 * ng
    c_in, c_specs, aliases = [], [], {}
    for j, c in enumerate(caches):
        a, s, al = _carried(None if carries is None else carries[j], c.shape, n_in + j)
        c_in, c_specs = c_in + a, c_specs + s
        aliases.update({k: 1 + j for k in al})
    outs = pl.pallas_call(
        functools.partial(_attn_decode_body, t_new=t_new), grid=(bsz,),
        in_specs=[pl.BlockSpec((1, tp, 2 * ATTN_W), lambda i: (i, 0, 0))]
                 + [pl.BlockSpec((1, tp, GROUP_W), lambda i: (i, 0, 0))] * ng
                 + [_resident((tp, GROUP_W)), _resident((tp, GROUP_W))] + cache_specs + c_specs,
        out_specs=[pl.BlockSpec((1, tp, ATTN_W), lambda i: (i, 0, 0))] + cache_specs,
        out_shape=[jax.ShapeDtypeStruct((bsz, tp, ATTN_W), F32)]
                  + [jax.ShapeDtypeStruct(c.shape, F32) for c in caches],
        scratch_shapes=[pltpu.VMEM((LANES, GROUP_W), F32)],
        input_output_aliases=aliases,
        compiler_params=_cparams("parallel"), name="attn_decode",
    )(qk, *vs, cos_t, sin_t, *caches, *c_in)
    return outs[0], outs[1:]


def _rwkv_body(cc_ref, prev_ref, s0_ref, mix_ref, w0_ref, wup_ref, a0_ref, aup_ref, gup_ref,
               kk_ref, ka_ref, rk_ref, lng_ref, lnb_ref, y_ref, s_ref, prev_scr, *, t_valid):
    c = pl.program_id(1)
    nb, chunk = cc_ref.shape[0], cc_ref.shape[1]

    @pl.when(c == 0)
    def _():
        s_ref[...] = s0_ref[...]
        for bi in range(nb):
            prev_scr[bi:bi + 1, :] = prev_ref[bi]

    row = lax.broadcasted_iota(jnp.int32, (chunk, 1), 0)
    ri = lax.broadcasted_iota(jnp.int32, (chunk, chunk), 0)
    ci = lax.broadcasted_iota(jnp.int32, (chunk, chunk), 1)
    incl, strict = ri >= ci, ri > ci
    tri = incl.astype(F32)
    masked = t_valid is not None
    if masked:
        valid = (c * chunk + row) < t_valid

    probs = []
    for bi in range(nb):
        probs += _rwkv_prepare(bi, cc_ref, prev_scr, s_ref, mix_ref, w0_ref, wup_ref, a0_ref, aup_ref, gup_ref,
                               kk_ref, ka_ref, tri, row, valid if masked else None)

    bf = lambda xs_: [x.astype(BF16) for x in xs_]
    a_ab = bf([jnp.where(strict, _dot_nt(p['a_t'], p['b_t']), 0.0) for p in probs])
    a_ak = bf([jnp.where(strict, _dot_nt(p['a_t'], p['k_t']), 0.0) for p in probs])
    a_rb = bf([jnp.where(incl, _dot_nt(p['r_t'], p['b_t']), 0.0) for p in probs])
    a_rk = bf([jnp.where(incl, _dot_nt(p['r_t'], p['k_t']), 0.0) for p in probs])
    u = [_dot_nt(p['a_t'], p['s0_b']) + _dot(m, p['v_b']) for p, m in zip(probs, a_ak)]
    o_part = [_dot_nt(p['r_t'], p['s0_b']) + _dot(m, p['v_b']) for p, m in zip(probs, a_rk)]
    s_part = [p['s0'] + _dot(p['v_h'].T.astype(BF16), p['k_t']) for p in probs]
    pw = a_ab
    n_levels = max((chunk - 1).bit_length(), 1)
    for lvl in range(n_levels):
        u_next = [ui + _dot(m, ui.astype(BF16)) for ui, m in zip(u, pw)]
        if lvl + 1 < n_levels:
            pw = bf([_dot(m, m) for m in pw])
        u = u_next
    o = [op + _dot(m, ui.astype(BF16)) for op, m, ui in zip(o_part, a_rb, u)]
    s_new = [(sp + _dot(ui.T.astype(BF16), p['b_t'])) * p['dec_all'] for sp, ui, p in zip(s_part, u, probs)]

    for bi in range(nb):
        ys = []
        for h in range(RWKV_HEADS):
            i = bi * RWKV_HEADS + h
            p, hl = probs[i], slice(h * HEAD_DIM, (h + 1) * HEAD_DIM)
            s_ref[bi, h] = s_new[i]
            mu = jnp.mean(o[i], axis=-1, keepdims=True)
            var = jnp.mean(jnp.square(o[i] - mu), axis=-1, keepdims=True)
            on = (o[i] - mu) * lax.rsqrt(var + LNX_EPS) * lng_ref[:, hl] + lnb_ref[:, hl]
            bonus = jnp.sum(p['r_h'] * p['k_h'] * rk_ref[:, hl], axis=-1, keepdims=True) * p['v_h']
            ys.append((on + bonus) * p['gate'])
        y_ref[bi] = jnp.concatenate(ys, axis=1)


def _rwkv_prepare(bi, cc_ref, prev_scr, s_ref, mix_ref, w0_ref, wup_ref, a0_ref, aup_ref, gup_ref,
                  kk_ref, ka_ref, tri, row, valid):
    chunk = cc_ref.shape[1]
    f = cc_ref[bi]
    shifted = jnp.where(row == 0, prev_scr[bi:bi + 1, :], pltpu.roll(f, 1, 0))
    prev_scr[bi:bi + 1, :] = f[chunk - 1:chunk, :]
    xs = f + (shifted - f) * mix_ref[...]

    r = xs[:, 0:RWKV_W]
    k = xs[:, RWKV_W:2 * RWKV_W]
    v = xs[:, 2 * RWKV_W:3 * RWKV_W]
    o1 = 3 * RWKV_W
    wlo = xs[:, o1:o1 + LORA_DECAY]
    alo = xs[:, o1 + LORA_DECAY:o1 + LORA_DECAY + LORA_AAA]
    glo = xs[:, o1 + LORA_DECAY + LORA_AAA:]
    w = -jax.nn.softplus(-(w0_ref[...] + _dot(jnp.tanh(wlo).astype(BF16), wup_ref[...]))) - 0.5
    logw = -jnp.exp(w)
    a = jax.nn.sigmoid(a0_ref[...] + _dot(alo.astype(BF16), aup_ref[...]))
    gate = _dot(jax.nn.sigmoid(glo).astype(BF16), gup_ref[...])
    kk_all = k * kk_ref[...]
    k = k * (1.0 + (a - 1.0) * ka_ref[...])

    if valid is not None:
        logw = jnp.where(valid, logw, 0.0)
    cum = jnp.dot(tri, logw, preferred_element_type=F32, precision=lax.Precision.HIGHEST)
    dec_incl, dec_excl, dec_inv = jnp.exp(cum), jnp.exp(cum - logw), jnp.exp(-cum)
    dec_all = dec_incl[chunk - 1:chunk, :]

    out = []
    for h in range(RWKV_HEADS):
        hl = slice(h * HEAD_DIM, (h + 1) * HEAD_DIM)
        kk = kk_all[:, hl]
        kk = kk / jnp.maximum(jnp.sqrt(jnp.sum(kk * kk, axis=-1, keepdims=True)), 1e-12)
        a_vec, b_vec, k_h = -kk, kk * a[:, hl], k[:, hl]
        if valid is not None:
            a_vec = jnp.where(valid, a_vec, 0.0)
            b_vec = jnp.where(valid, b_vec, 0.0)
            k_h = jnp.where(valid, k_h, 0.0)
        r_h, v_h = r[:, hl], v[:, hl]
        s0 = s_ref[bi, h]
        out.append(dict(
            a_t=(a_vec * dec_excl[:, hl]).astype(BF16), r_t=(r_h * dec_incl[:, hl]).astype(BF16),
            b_t=(b_vec * dec_inv[:, hl]).astype(BF16), k_t=(k_h * dec_inv[:, hl]).astype(BF16),
            v_b=v_h.astype(BF16), v_h=v_h, r_h=r_h, k_h=k_h, s0=s0, s0_b=s0.astype(BF16),
            dec_all=dec_all[:, hl], gate=gate[:, hl]))
    return out


_RWKV_PARAMS = ('shift_mix', 'w0', 'w_up', 'a0', 'a_up', 'g_up', 'k_k', 'k_a', 'r_k', 'lnx_g', 'lnx_b')


def _rwkv(l, cols_c, prev, s0, params, t_valid=None):
    bsz, t, _ = cols_c.shape
    chunk = _row_tile(t, RWKV_CHUNK)
    nb = _row_tile(bsz, RWKV_BATCH)
    state = pl.BlockSpec((nb, RWKV_HEADS, HEAD_DIM, HEAD_DIM), lambda b, c: (b, 0, 0, 0))
    return pl.pallas_call(
        functools.partial(_rwkv_body, t_valid=t_valid), grid=(bsz // nb, t // chunk),
        in_specs=[pl.BlockSpec((nb, chunk, C_COLS), lambda b, c: (b, c, 0)),
                  pl.BlockSpec((nb, 1, C_COLS), lambda b, c: (b, 0, 0)), state]
                 + [_layer(a, l) for a in params],
        out_specs=[pl.BlockSpec((nb, chunk, RWKV_W), lambda b, c: (b, c, 0)), state],
        out_shape=[jax.ShapeDtypeStruct((bsz, t, RWKV_W), F32),
                   jax.ShapeDtypeStruct(s0.shape, F32)],
        scratch_shapes=[pltpu.VMEM((nb, C_COLS), F32)],
        compiler_params=_cparams("parallel", "arbitrary"), name="rwkv",
    )(cols_c, prev, s0, *params)


def _xattn_body(x_ref, g_ref, wq_ref, wo_ref, mk_ref, mv_ref, o_ref):
    x = x_ref[0]
    h = _rms(x, g_ref[...]).astype(BF16)
    q = _dot(h, wq_ref[...]) * (XATTN_HEAD_DIM ** -0.5)
    outs = []
    for hh in range(XATTN_HEADS):
        hl = slice(hh * XATTN_HEAD_DIM, (hh + 1) * XATTN_HEAD_DIM)
        s = _dot_nt(q[:, hl].astype(BF16), mk_ref[0, :, hl].astype(BF16))
        m = s.max(axis=-1, keepdims=True)
        p = jnp.exp(s - m)
        den = p.sum(axis=-1, keepdims=True)
        outs.append(_dot(p.astype(BF16), mv_ref[0, :, hl].astype(BF16)) / den)
    o = jnp.concatenate(outs, axis=1).astype(BF16)
    o_ref[0] = x + _dot(o, wo_ref[...])


def _xattn(l, x, g, wq, wo, mk, mv):
    bsz, t, d = x.shape
    tq = _row_tile(t, ROW_TILE)
    nm = mk.shape[1]
    row = pl.BlockSpec((1, tq, d), lambda b, i: (b, i, 0))
    mem = pl.BlockSpec((1, nm, d), lambda b, i: (b, 0, 0))
    return pl.pallas_call(
        _xattn_body, grid=(bsz, t // tq),
        in_specs=[row, _layer(g, l), _layer(wq, l), _layer(wo, l), mem, mem],
        out_specs=row, out_shape=jax.ShapeDtypeStruct(x.shape, F32),
        compiler_params=_cparams("parallel", "parallel"), name="xattn",
    )(x, g, wq, wo, mk, mv)


def _final_norm_body(x_ref, g_ref, o_ref):
    o_ref[...] = _rms(x_ref[...], g_ref[...])


def _final_norm(x, g):
    m, d = x.shape
    tm = _row_tile(m, 2 * ROW_TILE)
    row = pl.BlockSpec((tm, d), lambda i: (i, 0))
    return pl.pallas_call(
        _final_norm_body, grid=(m // tm,), in_specs=[row, _resident((1, d))], out_specs=row,
        out_shape=jax.ShapeDtypeStruct((m, d), F32), compiler_params=_cparams("parallel"),
        name="final_norm",
    )(x, g.reshape(1, d))


def _pad_rows(a, rows):
    return jnp.pad(a, ((0, 0), (0, rows - a.shape[1]), (0, 0)))


def _window_layout(c):
    d, b, n, h, e = c.shape
    return jnp.transpose(c, (0, 1, 3, 4, 2)).reshape(d, b, h * e, n)


def _window_unlayout(c):
    d, b, he, n = c.shape
    return jnp.transpose(c.reshape(d, b, he // HEAD_DIM, HEAD_DIM, n), (0, 1, 4, 2, 3))


def _trunk(x, pos, P, mem, mem_k_cache, mem_v_cache, conv_cache, win_caches, shift_cache, wkv_cache):
    prompt = mem is not None
    bsz, t, d = x.shape
    ng = N_GROUPS
    tp = t if prompt else -(-t // SUBLANES) * SUBLANES
    cos_t, sin_t = _rope_tables(pos if prompt else pos[0] + jnp.arange(tp, dtype=pos.dtype))
    rwkv_params = [P['rwkv_' + n] for n in _RWKV_PARAMS]
    if not prompt:
        caches = [_window_layout(win_caches[g][j]) for j in range(2) for g in range(ng)]
    x = x.reshape(bsz * t, d)
    conv_new, shift_new, wkv_new, memk_new, memv_new = [], [], [], [], []
    win_new = None
    for l in range(DEPTH):
        x = _ffn(l, x, P['ffn1_norm'], P['ffn1_w_gate'], P['ffn1_w_up'], P['ffn1_w_down'])

        cols = _norm_proj(l, x, P['mix_norm'], P['w_in'], (A_COLS, 2 * ATTN_W) + (GROUP_W,) * ng + (C_COLS,))
        cols = [c.reshape(bsz, t, c.shape[1]) for c in cols]
        cols_a, qk, vs, cols_c = cols[0], cols[1], cols[2:2 + ng], cols[-1]

        hist = jnp.zeros((bsz, CONV_HIST, CONV_CH), F32) if prompt else conv_cache[l]
        y_a, conv_state = _conv_mixer(l, cols_a, hist, P['conv_w'], P['conv_b'], P['conv_ln_g'], P['conv_ln_b'])
        conv_new.append(conv_state)

        if prompt:
            y_bs, win_new = _attn_prompt(l, qk, vs, cos_t, sin_t, win_new)
            y_bs = [y.reshape(bsz * t, GROUP_W) for y in y_bs]
        else:
            y_b, win_new = _attn_decode(l, _pad_rows(qk, tp), [_pad_rows(v, tp) for v in vs],
                                        cos_t, sin_t, caches, win_new, t)
            y_bs = [y_b[:, :t].reshape(bsz * t, ATTN_W)]

        if prompt:
            prev = jnp.zeros((bsz, 1, C_COLS), F32)
            s0 = jnp.zeros((bsz, RWKV_HEADS, HEAD_DIM, HEAD_DIM), F32)
            y_c, s_new = _rwkv(l, cols_c, prev, s0, rwkv_params)
        else:
            y_c, s_new = _rwkv(l, _pad_rows(cols_c, tp), shift_cache[l][:, None, :], wkv_cache[l], rwkv_params,
                               t_valid=None if tp == t else t)
            y_c = y_c[:, :t]
        shift_new.append(cols_c[:, -1])
        wkv_new.append(s_new)

        x = _out_proj(l, x, [y_a.reshape(bsz * t, CONV_CH)] + y_bs + [y_c.reshape(bsz * t, RWKV_W)], P['w_out'])

        if prompt:
            nm = mem.shape[1]
            mk, mv = _norm_proj(l, mem.reshape(bsz * nm, d), P['mem_norm'], P['xattn_w_kv'], (d, d))
            mk, mv = mk.reshape(bsz, nm, d), mv.reshape(bsz, nm, d)
            memk_new.append(mk.reshape(bsz, nm, XATTN_HEADS, XATTN_HEAD_DIM))
            memv_new.append(mv.reshape(bsz, nm, XATTN_HEADS, XATTN_HEAD_DIM))
        else:
            nm = mem_k_cache.shape[2]
            mk = mem_k_cache[l].reshape(bsz, nm, d)
            mv = mem_v_cache[l].reshape(bsz, nm, d)
        x = _xattn(l, x.reshape(bsz, t, d), P['xattn_norm'], P['xattn_w_q'], P['xattn_w_o'],
                   mk, mv).reshape(bsz * t, d)

        x = _ffn(l, x, P['ffn2_norm'], P['ffn2_w_gate'], P['ffn2_w_up'], P['ffn2_w_down'])

    y = _final_norm(x, P['final_norm']).reshape(bsz, t, d)
    stack = jnp.stack
    outs = [y, stack(conv_new)]
    for g in range(ng):
        outs += [_window_unlayout(win_new[g]), _window_unlayout(win_new[ng + g])]
    outs += [stack(shift_new), stack(wkv_new)]
    if prompt:
        outs += [stack(memk_new), stack(memv_new)]
    return outs


_MATMUL_WEIGHTS = ('ffn1_w_gate', 'ffn1_w_up', 'ffn1_w_down', 'w_in', 'w_out', 'rwkv_w_up', 'rwkv_a_up',
                   'rwkv_g_up', 'xattn_w_q', 'xattn_w_kv', 'xattn_w_o', 'ffn2_w_gate', 'ffn2_w_up',
                   'ffn2_w_down')
_ROW_VECTORS = ('ffn1_norm', 'mix_norm', 'conv_b', 'conv_ln_g', 'conv_ln_b', 'rwkv_shift_mix', 'rwkv_w0',
                'rwkv_a0', 'rwkv_k_k', 'rwkv_k_a', 'rwkv_r_k', 'rwkv_lnx_g', 'rwkv_lnx_b', 'xattn_norm',
                'mem_norm', 'ffn2_norm')


def kernel(x_prompt, x_sample, cache_conv, cache_win1_k, cache_win1_v, cache_win2_k, cache_win2_v,
           cache_win3_k, cache_win3_v, state_shift, state_wkv, cache_mem_k, cache_mem_v, mem_prompt,
           ffn1_norm, ffn1_w_gate, ffn1_w_up, ffn1_w_down, mix_norm, w_in, w_out,
           conv_w, conv_b, conv_ln_g, conv_ln_b,
           rwkv_shift_mix, rwkv_w0, rwkv_w_up, rwkv_a0, rwkv_a_up, rwkv_g_up, rwkv_k_k, rwkv_k_a,
           rwkv_r_k, rwkv_lnx_g, rwkv_lnx_b,
           xattn_norm, mem_norm, xattn_w_q, xattn_w_kv, xattn_w_o,
           ffn2_norm, ffn2_w_gate, ffn2_w_up, ffn2_w_down, final_norm):
    P = dict(ffn1_norm=ffn1_norm, ffn1_w_gate=ffn1_w_gate, ffn1_w_up=ffn1_w_up, ffn1_w_down=ffn1_w_down,
             mix_norm=mix_norm, w_in=w_in, w_out=w_out,
             conv_w=conv_w, conv_b=conv_b, conv_ln_g=conv_ln_g, conv_ln_b=conv_ln_b,
             rwkv_shift_mix=rwkv_shift_mix, rwkv_w0=rwkv_w0, rwkv_w_up=rwkv_w_up, rwkv_a0=rwkv_a0,
             rwkv_a_up=rwkv_a_up, rwkv_g_up=rwkv_g_up, rwkv_k_k=rwkv_k_k, rwkv_k_a=rwkv_k_a,
             rwkv_r_k=rwkv_r_k, rwkv_lnx_g=rwkv_lnx_g, rwkv_lnx_b=rwkv_lnx_b,
             xattn_norm=xattn_norm, mem_norm=mem_norm, xattn_w_q=xattn_w_q, xattn_w_kv=xattn_w_kv,
             xattn_w_o=xattn_w_o, ffn2_norm=ffn2_norm, ffn2_w_gate=ffn2_w_gate, ffn2_w_up=ffn2_w_up,
             ffn2_w_down=ffn2_w_down, final_norm=final_norm)
    for name in _MATMUL_WEIGHTS:
        P[name] = P[name].astype(BF16)
    for name in _ROW_VECTORS:
        P[name] = P[name].reshape(DEPTH, 1, -1)

    pos_p = jnp.arange(x_prompt.shape[1], dtype=jnp.int32)
    outs_p = _trunk(x_prompt, pos_p, P, mem_prompt, None, None, None, None, None, None)

    pos_s = PAST_LEN + jnp.arange(x_sample.shape[1], dtype=jnp.int32)
    win_caches = [(cache_win1_k, cache_win1_v), (cache_win2_k, cache_win2_v), (cache_win3_k, cache_win3_v)]
    outs_s = _trunk(x_sample, pos_s, P, None, cache_mem_k, cache_mem_v, cache_conv, win_caches,
                    state_shift, state_wkv)

    return (outs_p[0], outs_s[0], *outs_p[1:], *outs_s[1:])
```

```python
import functools

import jax
import jax.numpy as jnp
from jax import lax
from jax.experimental import pallas as pl
from jax.experimental.pallas import tpu as pltpu

F32 = jnp.float32
BF16 = jnp.bfloat16

D_MODEL = 1024
DEPTH = 4
HEAD_DIM = 64
CONV_CH = 256
CONV_WIDTH = 31
ATTN_GROUPS = ((128, 1), (512, 4), (2048, 16))
N_GROUPS = len(ATTN_GROUPS)
GROUP_W = 2 * HEAD_DIM
ATTN_W = GROUP_W * N_GROUPS
ATTN_SCALE = HEAD_DIM ** -0.5
ATTN_BLOCK = 128
ATTN_UNROLL = 4
RWKV_HEADS = 6
RWKV_W = RWKV_HEADS * HEAD_DIM
LORA_DECAY = 64
LORA_AAA = 64
LORA_GATE = 128
LNX_EPS = 64e-5
ROPE_THETA = 10000.0
XATTN_HEADS = 4
XATTN_HEAD_DIM = D_MODEL // XATTN_HEADS
NORM_EPS = 1e-6
A_COLS = 2 * CONV_CH
B_COLS = 3 * ATTN_W
C_COLS = 3 * RWKV_W + LORA_DECAY + LORA_AAA + LORA_GATE
PAST_LEN = 8192

VMEM_LIMIT_BYTES = 56 * 1024 * 1024
LANES = 128
SUBLANES = 8
ROW_TILE = 512
FFN_TF = 256
RWKV_CHUNK = 64
RWKV_BATCH = 4


def _cparams(*sem):
    return pltpu.CompilerParams(dimension_semantics=sem, vmem_limit_bytes=VMEM_LIMIT_BYTES)


def _resident(shape):
    nd = len(shape)
    return pl.BlockSpec(shape, lambda *_: (0,) * nd, pipeline_mode=pl.Buffered(1))


def _layer(arr, l):
    nd = arr.ndim
    return pl.BlockSpec((None,) + arr.shape[1:], lambda *_: (l,) + (0,) * (nd - 1),
                        pipeline_mode=pl.Buffered(1))


def _carried(carry, shape, n_in):
    if carry is None:
        return [], [], {}
    assert carry.shape == shape
    return [carry], [pl.BlockSpec(memory_space=pl.ANY)], {n_in: None}


def _rms(x, g):
    return x * lax.rsqrt(jnp.mean(x * x, axis=-1, keepdims=True) + NORM_EPS) * g


def _dot(a, b):
    return jnp.dot(a, b, preferred_element_type=F32)


def _dot_nt(a, b):
    return lax.dot_general(a, b, (((1,), (1,)), ((), ())), preferred_element_type=F32)


def _row_tile(m, target):
    t = min(m, target)
    assert m % t == 0, (m, t)
    return t


def _ffn_body(x_ref, g_ref, wg_ref, wu_ref, wd_ref, o_ref):
    x = x_ref[...]
    h = _rms(x, g_ref[...]).astype(BF16)
    acc = jnp.zeros(x.shape, F32)
    for j in range(wg_ref.shape[1] // FFN_TF):
        sl = slice(j * FFN_TF, (j + 1) * FFN_TF)
        gate = _dot(h, wg_ref[:, sl])
        up = _dot(h, wu_ref[:, sl])
        act = (gate * jax.nn.sigmoid(gate) * up).astype(BF16)
        acc = acc + _dot(act, wd_ref[sl, :])
    o_ref[...] = x + 0.5 * acc


def _ffn(l, x, g, wg, wu, wd):
    m, d = x.shape
    tm = _row_tile(m, ROW_TILE)
    row = pl.BlockSpec((tm, d), lambda i: (i, 0))
    return pl.pallas_call(
        _ffn_body, grid=(m // tm,),
        in_specs=[row, _layer(g, l), _layer(wg, l), _layer(wu, l), _layer(wd, l)],
        out_specs=row, out_shape=jax.ShapeDtypeStruct((m, d), F32),
        compiler_params=_cparams("parallel"), name="ffn",
    )(x, g, wg, wu, wd)


def _norm_proj_body(x_ref, g_ref, w_ref, *o_refs):
    h = _rms(x_ref[...], g_ref[...]).astype(BF16)
    off = 0
    for o_ref in o_refs:
        n = o_ref.shape[1]
        o_ref[...] = _dot(h, w_ref[:, off:off + n])
        off += n


def _norm_proj(l, x, g, w, splits):
    m, d = x.shape
    tm = _row_tile(m, ROW_TILE)
    return pl.pallas_call(
        _norm_proj_body, grid=(m // tm,),
        in_specs=[pl.BlockSpec((tm, d), lambda i: (i, 0)), _layer(g, l), _layer(w, l)],
        out_specs=[pl.BlockSpec((tm, n), lambda i: (i, 0)) for n in splits],
        out_shape=[jax.ShapeDtypeStruct((m, n), F32) for n in splits],
        compiler_params=_cparams("parallel"), name="norm_proj",
    )(x, g, w)


def _out_proj_body(x_ref, *refs):
    y_refs, w_ref, o_ref = refs[:-2], refs[-2], refs[-1]
    acc = x_ref[...]
    off = 0
    for y_ref in y_refs:
        n = y_ref.shape[1]
        acc = acc + _dot(y_ref[...].astype(BF16), w_ref[off:off + n, :])
        off += n
    o_ref[...] = acc


def _out_proj(l, x, ys, w):
    m, d = x.shape
    tm = _row_tile(m, ROW_TILE)
    row = lambda n: pl.BlockSpec((tm, n), lambda i: (i, 0))
    return pl.pallas_call(
        _out_proj_body, grid=(m // tm,),
        in_specs=[row(d)] + [row(y.shape[1]) for y in ys] + [_layer(w, l)],
        out_specs=row(d), out_shape=jax.ShapeDtypeStruct((m, d), F32),
        compiler_params=_cparams("parallel"), name="out_proj",
    )(x, *ys, w)


CONV_HIST = CONV_WIDTH - 1
CONV_PAD = 32


def _conv_body(ca_ref, hist_ref, w_ref, b_ref, lg_ref, lb_ref, y_ref, st_ref, u_scr, *, chunk):
    t = ca_ref.shape[1]
    x = ca_ref[0]
    u_scr[CONV_PAD - CONV_HIST:CONV_PAD, :] = hist_ref[0]
    u_scr[CONV_PAD:CONV_PAD + t, :] = x[:, :CONV_CH] * jax.nn.sigmoid(x[:, CONV_CH:])
    st_ref[0] = u_scr[CONV_PAD + t - CONV_HIST:CONV_PAD + t, :]
    base = CONV_PAD - CONV_HIST
    for c in range(t // chunk):
        acc = jnp.zeros((chunk, CONV_CH), F32) + b_ref[...]
        for j in range(CONV_WIDTH):
            r0 = c * chunk + base + j
            acc = acc + u_scr[r0:r0 + chunk, :] * w_ref[j:j + 1, :]
        mu = jnp.mean(acc, axis=-1, keepdims=True)
        var = jnp.mean(jnp.square(acc - mu), axis=-1, keepdims=True)
        yn = (acc - mu) * lax.rsqrt(var + 1e-5) * lg_ref[...] + lb_ref[...]
        y_ref[0, c * chunk:(c + 1) * chunk, :] = yn * jax.nn.sigmoid(yn)


def _conv_mixer(l, cols_a, hist, w, b, lg, lb):
    bsz, t, _ = cols_a.shape
    chunk = _row_tile(t, 128)
    return pl.pallas_call(
        functools.partial(_conv_body, chunk=chunk), grid=(bsz,),
        in_specs=[pl.BlockSpec((1, t, A_COLS), lambda i: (i, 0, 0)),
                  pl.BlockSpec((1, CONV_HIST, CONV_CH), lambda i: (i, 0, 0)),
                  _layer(w, l), _layer(b, l), _layer(lg, l), _layer(lb, l)],
        out_specs=[pl.BlockSpec((1, t, CONV_CH), lambda i: (i, 0, 0)),
                   pl.BlockSpec((1, CONV_HIST, CONV_CH), lambda i: (i, 0, 0))],
        out_shape=[jax.ShapeDtypeStruct((bsz, t, CONV_CH), F32),
                   jax.ShapeDtypeStruct((bsz, CONV_HIST, CONV_CH), F32)],
        scratch_shapes=[pltpu.VMEM((CONV_PAD + t, CONV_CH), F32)],
        compiler_params=_cparams("parallel"), name="conv_mixer",
    )(cols_a, hist, w, b, lg, lb)


def _rope_tables(pos):
    half = HEAD_DIM // 2
    inv = jnp.power(ROPE_THETA, -jnp.arange(half, dtype=F32) / half)
    ang = pos.astype(F32)[:, None] * inv[None, :]
    cos, sin = jnp.cos(ang), jnp.sin(ang)
    cos_t = jnp.concatenate([cos, cos], axis=-1)
    sin_t = jnp.concatenate([-sin, sin], axis=-1)
    reps = GROUP_W // HEAD_DIM
    return jnp.tile(cos_t, (1, reps)), jnp.tile(sin_t, (1, reps))


def _rope(x, cos, sin):
    half = HEAD_DIM // 2
    lane = lax.broadcasted_iota(jnp.int32, x.shape, 1)
    first_half = (lane & (HEAD_DIM - 1)) < half
    partner = jnp.where(first_half, pltpu.roll(x, GROUP_W - half, 1), pltpu.roll(x, half, 1))
    return x * cos + partner * sin


def _softmax_parts(scores):
    m = scores[0].max(axis=-1, keepdims=True)
    for s in scores[1:]:
        m = jnp.maximum(m, s.max(axis=-1, keepdims=True))
    ps = [jnp.exp(s - m) for s in scores]
    den = ps[0].sum(axis=-1, keepdims=True)
    for p in ps[1:]:
        den = den + p.sum(axis=-1, keepdims=True)
    return m, ps, den


def _lanes_per_head(vals):
    return jnp.concatenate([jnp.broadcast_to(v, (v.shape[0], HEAD_DIM)) for v in vals], axis=1)


def _mix_dilations(parts):
    m_all = functools.reduce(jnp.maximum, [p[1] for p in parts])
    scl = [jnp.exp(p[1] - m_all) for p in parts]
    tot = functools.reduce(lambda a, b: a + b, [p[2] * s for p, s in zip(parts, scl)])
    return [p[0] * (s / tot) for p, s in zip(parts, scl)]


def _attn_prompt_body(*refs):
    ng = N_GROUPS
    qk_ref, v_refs, cos_ref, sin_ref = refs[0], refs[1:1 + ng], refs[1 + ng], refs[2 + ng]
    rest = refs[3 + ng:]
    n_carry = len(rest) - (3 * ng + 9)
    rest = rest[n_carry:]
    y_refs, wk_refs, wv_refs = rest[:ng], rest[ng:2 * ng], rest[2 * ng:3 * ng]
    q_scr, kr_scr, pq_scr, pk_scr, pv_scr, po_scr, pm_scr, pd_scr, md_scr = rest[3 * ng:]
    t = qk_ref.shape[1]
    rc = min(t, 256)
    nblk = t // ATTN_BLOCK

    def rope_chunk(i, carry):
        rows = pl.ds(pl.multiple_of(i * rc, rc), rc)
        cs, sn = cos_ref[rows, :], sin_ref[rows, :]
        for g in range(ng):
            q_scr[g, rows, :] = _rope(qk_ref[0, rows, g * GROUP_W:(g + 1) * GROUP_W], cs, sn) * ATTN_SCALE
            kr_scr[g, rows, :] = _rope(qk_ref[0, rows, ATTN_W + g * GROUP_W:ATTN_W + (g + 1) * GROUP_W], cs, sn)
        return carry

    lax.fori_loop(0, t // rc, rope_chunk, 0)

    qi = lax.broadcasted_iota(jnp.int32, (ATTN_BLOCK, ATTN_BLOCK), 0)
    kj = lax.broadcasted_iota(jnp.int32, (ATTN_BLOCK, ATTN_BLOCK), 1)

    for g, (win, dil) in enumerate(ATTN_GROUPS):
        assert win // dil == ATTN_BLOCK and t % (dil * ATTN_BLOCK) == 0
        sub_len = t // dil
        sub_blocks = sub_len // ATTN_BLOCK

        keep = wk_refs[g].shape[-1]
        for j in range(keep // LANES):
            src = slice(t - keep + j * LANES, t - keep + (j + 1) * LANES)
            dst = slice(j * LANES, (j + 1) * LANES)
            wk_refs[g][:, dst] = kr_scr[g, src, :].T
            wv_refs[g][:, dst] = v_refs[g][0, src, :].T

        for r in range(dil):
            rows = pl.ds(r, sub_len, stride=dil) if dil > 1 else slice(0, t)
            dst = slice(r * sub_len, (r + 1) * sub_len)
            pq_scr[dst, :] = q_scr[g, rows, :].astype(BF16)
            pk_scr[dst, :] = kr_scr[g, rows, :].astype(BF16)
            pv_scr[dst, :] = v_refs[g][0, rows, :].astype(BF16)

        nh = GROUP_W // HEAD_DIM
        add = lambda a, b: a + b

        def block_group(i, carry):
            owns, chains = [], []
            for u in range(ATTN_UNROLL):
                b = i * ATTN_UNROLL + u
                own = pl.ds(pl.multiple_of(b * ATTN_BLOCK, ATTN_BLOCK), ATTN_BLOCK)
                owns.append(own)
                q, k_own, v_own = pq_scr[own, :], pk_scr[own, :], pv_scr[own, :]
                masks = [kj <= qi]
                if sub_blocks > 1:
                    prev = pl.ds(pl.multiple_of(jnp.maximum(b - 1, 0) * ATTN_BLOCK, ATTN_BLOCK), ATTN_BLOCK)
                    k_prev, v_prev = pk_scr[prev, :], pv_scr[prev, :]
                    masks.append(jnp.logical_and(kj >= qi, lax.rem(b, sub_blocks) > 0))
                for h in range(nh):
                    hl = slice(h * HEAD_DIM, (h + 1) * HEAD_DIM)
                    ks, vs = [k_own[:, hl]], [v_own[:, hl]]
                    if sub_blocks > 1:
                        ks.append(k_prev[:, hl])
                        vs.append(v_prev[:, hl])
                    chains.append((q[:, hl], ks, vs, masks))
            scores = [[jnp.where(mk, _dot_nt(q, kk), -jnp.inf) for mk, kk in zip(masks, ks)]
                      for q, ks, _, masks in chains]
            ms = [functools.reduce(jnp.maximum, [s.max(axis=-1, keepdims=True) for s in sc]) for sc in scores]
            ps = [[jnp.exp(s - m) for s in sc] for sc, m in zip(scores, ms)]
            ds = [functools.reduce(add, [p.sum(axis=-1, keepdims=True) for p in pp]) for pp in ps]
            os = [functools.reduce(add, [_dot(p.astype(BF16), vv) for p, vv in zip(pp, c[2])])
                  for pp, c in zip(ps, chains)]
            for u, own in enumerate(owns):
                sl = slice(u * nh, (u + 1) * nh)
                po_scr[own, :] = jnp.concatenate(os[sl], axis=1)
                pm_scr[own, :] = _lanes_per_head(ms[sl])
                pd_scr[own, :] = _lanes_per_head(ds[sl])
            return carry

        assert nblk % ATTN_UNROLL == 0
        lax.fori_loop(0, nblk // ATTN_UNROLL, block_group, 0)

        for r in range(dil):
            rows = pl.ds(r, sub_len, stride=dil) if dil > 1 else slice(0, t)
            src = slice(r * sub_len, (r + 1) * sub_len)
            y_refs[g][0, rows, :] = po_scr[src, :]
            md_scr[g, rows, :] = pm_scr[src, :]
            md_scr[ng + g, rows, :] = pd_scr[src, :]

    def combine_chunk(i, carry):
        rows = pl.ds(pl.multiple_of(i * rc, rc), rc)
        mixed = _mix_dilations([(y_refs[g][0, rows, :], md_scr[g, rows, :], md_scr[ng + g, rows, :])
                                for g in range(ng)])
        for g in range(ng):
            y_refs[g][0, rows, :] = mixed[g]
        return carry

    lax.fori_loop(0, t // rc, combine_chunk, 0)


def _attn_prompt(l, qk, vs, cos_t, sin_t, carries):
    bsz, t, _ = qk.shape
    ng = N_GROUPS
    grp = pl.BlockSpec((1, t, GROUP_W), lambda i: (i, 0, 0))
    grp_in = pl.BlockSpec((1, t, GROUP_W), lambda i: (i, 0, 0), pipeline_mode=pl.Buffered(1))
    keeps = [min(win, t) for win, _ in ATTN_GROUPS]
    win_shapes = [(DEPTH, bsz, GROUP_W, kp) for kp in keeps] * 2
    win_specs = [pl.BlockSpec((None, None, GROUP_W, kp), lambda i: (l, i, 0, 0)) for kp in keeps] * 2
    n_in = 3 + ng
    c_in, c_specs, aliases = [], [], {}
    for j, shp in enumerate(win_shapes):
        a, s, al = _carried(None if carries is None else carries[j], shp, n_in + j)
        c_in, c_specs = c_in + a, c_specs + s
        aliases.update({k: ng + j for k in al})
    outs = pl.pallas_call(
        _attn_prompt_body, grid=(bsz,),
        in_specs=[pl.BlockSpec((1, t, 2 * ATTN_W), lambda i: (i, 0, 0), pipeline_mode=pl.Buffered(1))]
                 + [grp_in] * ng + [_resident((t, GROUP_W)), _resident((t, GROUP_W))] + c_specs,
        out_specs=[grp] * ng + win_specs,
        out_shape=[jax.ShapeDtypeStruct((bsz, t, GROUP_W), F32)] * ng
                  + [jax.ShapeDtypeStruct(s, F32) for s in win_shapes],
        scratch_shapes=[pltpu.VMEM((ng, t, GROUP_W), F32), pltpu.VMEM((ng, t, GROUP_W), F32),
                        pltpu.VMEM((t, GROUP_W), BF16), pltpu.VMEM((t, GROUP_W), BF16),
                        pltpu.VMEM((t, GROUP_W), BF16),
                        pltpu.VMEM((t, GROUP_W), F32), pltpu.VMEM((t, GROUP_W), F32),
                        pltpu.VMEM((t, GROUP_W), F32),
                        pltpu.VMEM((2 * ng, t, GROUP_W), F32)],
        input_output_aliases=aliases,
        compiler_params=_cparams("parallel"), name="attn_prompt",
    )(qk, *vs, cos_t, sin_t, *c_in)
    return outs[:ng], outs[ng:]


def _attn_decode_body(*refs, t_new):
    ng = N_GROUPS
    qk_ref, v_refs, cos_ref, sin_ref = refs[0], refs[1:1 + ng], refs[1 + ng], refs[2 + ng]
    cache_refs = refs[3 + ng:3 + 3 * ng]
    rest = refs[3 + 3 * ng:]
    n_carry = len(rest) - (1 + 2 * ng + 1)
    rest = rest[n_carry:]
    y_ref, out_refs, pad_scr = rest[0], rest[1:1 + 2 * ng], rest[-1]
    tp = qk_ref.shape[1]
    cs, sn = cos_ref[...], sin_ref[...]
    tail_lane = lax.broadcasted_iota(jnp.int32, (GROUP_W, LANES), 1)
    parts = []
    for g, (win, dil) in enumerate(ATTN_GROUPS):
        assert dil & (dil - 1) == 0
        n_old = cache_refs[g].shape[-1]
        q = (_rope(qk_ref[0, :, g * GROUP_W:(g + 1) * GROUP_W], cs, sn) * ATTN_SCALE).astype(BF16)
        k_new = _rope(qk_ref[0, :, ATTN_W + g * GROUP_W:ATTN_W + (g + 1) * GROUP_W], cs, sn)
        v_new = v_refs[g][0]
        old = []
        for new_rows, c_ref, o_ref in ((k_new, cache_refs[g], out_refs[g]),
                                       (v_new, cache_refs[ng + g], out_refs[ng + g])):
            c_old = c_ref[...]
            moved = pltpu.roll(c_old, n_old - t_new, 1)
            pad_scr[...] = jnp.zeros(pad_scr.shape, F32)
            pad_scr[LANES - t_new:LANES, :] = new_rows[0:t_new, :]
            tail = jnp.where(tail_lane >= LANES - t_new, pad_scr[...].T, moved[:, n_old - LANES:])
            if n_old > LANES:
                o_ref[:, 0:n_old - LANES] = moved[:, 0:n_old - LANES]
            o_ref[:, n_old - LANES:] = tail
            old.append(c_old.astype(BF16))
        k_old, v_old = old
        dist_old = (n_old + lax.broadcasted_iota(jnp.int32, (tp, n_old), 0)
                    - lax.broadcasted_iota(jnp.int32, (tp, n_old), 1))
        ok_old = jnp.logical_and(dist_old <= win, (dist_old & (dil - 1)) == 0)
        dist_new = (lax.broadcasted_iota(jnp.int32, (tp, tp), 0) - lax.broadcasted_iota(jnp.int32, (tp, tp), 1))
        ok_new = jnp.logical_and(jnp.logical_and(dist_new >= 0, dist_new <= win), (dist_new & (dil - 1)) == 0)
        os, ms, ds = [], [], []
        for h in range(GROUP_W // HEAD_DIM):
            hl = slice(h * HEAD_DIM, (h + 1) * HEAD_DIM)
            s_old = jnp.where(ok_old, _dot(q[:, hl], k_old[hl, :]), -jnp.inf)
            s_new = jnp.where(ok_new, _dot_nt(q[:, hl], k_new[:, hl].astype(BF16)), -jnp.inf)
            m, (p_old, p_new), den = _softmax_parts([s_old, s_new])
            os.append(_dot_nt(p_old.astype(BF16), v_old[hl, :]) + _dot(p_new.astype(BF16), v_new[:, hl].astype(BF16)))
            ms.append(m)
            ds.append(den)
        parts.append((jnp.concatenate(os, axis=1), _lanes_per_head(ms), _lanes_per_head(ds)))
    y_ref[0] = jnp.concatenate(_mix_dilations(parts), axis=1)


def _attn_decode(l, qk, vs, cos_t, sin_t, caches, carries, t_new):
    bsz, tp, _ = qk.shape
    ng = N_GROUPS
    cache_specs = [pl.BlockSpec((None, None, GROUP_W, c.shape[-1]), lambda i: (l, i, 0, 0)) for c in caches]
    n_in = 3 + 3 * ng
    c_in, c_specs, aliases = [], [], {}
    for j, c in enumerate(caches):
        a, s, al = _carried(None if carries is None else carries[j], c.shape, n_in + j)
        c_in, c_specs = c_in + a, c_specs + s
        aliases.update({k: 1 + j for k in al})
    outs = pl.pallas_call(
        functools.partial(_attn_decode_body, t_new=t_new), grid=(bsz,),
        in_specs=[pl.BlockSpec((1, tp, 2 * ATTN_W), lambda i: (i, 0, 0))]
                 + [pl.BlockSpec((1, tp, GROUP_W), lambda i: (i, 0, 0))] * ng
                 + [_resident((tp, GROUP_W)), _resident((tp, GROUP_W))] + cache_specs + c_specs,
        out_specs=[pl.BlockSpec((1, tp, ATTN_W), lambda i: (i, 0, 0))] + cache_specs,
        out_shape=[jax.ShapeDtypeStruct((bsz, tp, ATTN_W), F32)]
                  + [jax.ShapeDtypeStruct(c.shape, F32) for c in caches],
        scratch_shapes=[pltpu.VMEM((LANES, GROUP_W), F32)],
        input_output_aliases=aliases,
        compiler_params=_cparams("parallel"), name="attn_decode",
    )(qk, *vs, cos_t, sin_t, *caches, *c_in)
    return outs[0], outs[1:]


def _rwkv_body(cc_ref, prev_ref, s0_ref, mix_ref, w0_ref, wup_ref, a0_ref, aup_ref, gup_ref,
               kk_ref, ka_ref, rk_ref, lng_ref, lnb_ref, y_ref, s_ref, prev_scr, *, t_valid):
    c = pl.program_id(1)
    nb, chunk = cc_ref.shape[0], cc_ref.shape[1]

    @pl.when(c == 0)
    def _():
        s_ref[...] = s0_ref[...]
        for bi in range(nb):
            prev_scr[bi:bi + 1, :] = prev_ref[bi]

    row = lax.broadcasted_iota(jnp.int32, (chunk, 1), 0)
    ri = lax.broadcasted_iota(jnp.int32, (chunk, chunk), 0)
    ci = lax.broadcasted_iota(jnp.int32, (chunk, chunk), 1)
    incl, strict = ri >= ci, ri > ci
    tri = incl.astype(F32)
    masked = t_valid is not None
    if masked:
        valid = (c * chunk + row) < t_valid

    probs = []
    for bi in range(nb):
        probs += _rwkv_prepare(bi, cc_ref, prev_scr, s_ref, mix_ref, w0_ref, wup_ref, a0_ref, aup_ref, gup_ref,
                               kk_ref, ka_ref, tri, row, valid if masked else None)

    bf = lambda xs_: [x.astype(BF16) for x in xs_]
    a_ab = bf([jnp.where(strict, _dot_nt(p['a_t'], p['b_t']), 0.0) for p in probs])
    a_ak = bf([jnp.where(strict, _dot_nt(p['a_t'], p['k_t']), 0.0) for p in probs])
    a_rb = bf([jnp.where(incl, _dot_nt(p['r_t'], p['b_t']), 0.0) for p in probs])
    a_rk = bf([jnp.where(incl, _dot_nt(p['r_t'], p['k_t']), 0.0) for p in probs])
    u = [_dot_nt(p['a_t'], p['s0_b']) + _dot(m, p['v_b']) for p, m in zip(probs, a_ak)]
    o_part = [_dot_nt(p['r_t'], p['s0_b']) + _dot(m, p['v_b']) for p, m in zip(probs, a_rk)]
    s_part = [p['s0'] + _dot(p['v_h'].T.astype(BF16), p['k_t']) for p in probs]
    pw = a_ab
    n_levels = max((chunk - 1).bit_length(), 1)
    for lvl in range(n_levels):
        u_next = [ui + _dot(m, ui.astype(BF16)) for ui, m in zip(u, pw)]
        if lvl + 1 < n_levels:
            pw = bf([_dot(m, m) for m in pw])
        u = u_next
    o = [op + _dot(m, ui.astype(BF16)) for op, m, ui in zip(o_part, a_rb, u)]
    s_new = [(sp + _dot(ui.T.astype(BF16), p['b_t'])) * p['dec_all'] for sp, ui, p in zip(s_part, u, probs)]

    for bi in range(nb):
        ys = []
        for h in range(RWKV_HEADS):
            i = bi * RWKV_HEADS + h
            p, hl = probs[i], slice(h * HEAD_DIM, (h + 1) * HEAD_DIM)
            s_ref[bi, h] = s_new[i]
            mu = jnp.mean(o[i], axis=-1, keepdims=True)
            var = jnp.mean(jnp.square(o[i] - mu), axis=-1, keepdims=True)
            on = (o[i] - mu) * lax.rsqrt(var + LNX_EPS) * lng_ref[:, hl] + lnb_ref[:, hl]
            bonus = jnp.sum(p['r_h'] * p['k_h'] * rk_ref[:, hl], axis=-1, keepdims=True) * p['v_h']
            ys.append((on + bonus) * p['gate'])
        y_ref[bi] = jnp.concatenate(ys, axis=1)


def _rwkv_prepare(bi, cc_ref, prev_scr, s_ref, mix_ref, w0_ref, wup_ref, a0_ref, aup_ref, gup_ref,
                  kk_ref, ka_ref, tri, row, valid):
    chunk = cc_ref.shape[1]
    f = cc_ref[bi]
    shifted = jnp.where(row == 0, prev_scr[bi:bi + 1, :], pltpu.roll(f, 1, 0))
    prev_scr[bi:bi + 1, :] = f[chunk - 1:chunk, :]
    xs = f + (shifted - f) * mix_ref[...]

    r = xs[:, 0:RWKV_W]
    k = xs[:, RWKV_W:2 * RWKV_W]
    v = xs[:, 2 * RWKV_W:3 * RWKV_W]
    o1 = 3 * RWKV_W
    wlo = xs[:, o1:o1 + LORA_DECAY]
    alo = xs[:, o1 + LORA_DECAY:o1 + LORA_DECAY + LORA_AAA]
    glo = xs[:, o1 + LORA_DECAY + LORA_AAA:]
    w = -jax.nn.softplus(-(w0_ref[...] + _dot(jnp.tanh(wlo).astype(BF16), wup_ref[...]))) - 0.5
    logw = -jnp.exp(w)
    a = jax.nn.sigmoid(a0_ref[...] + _dot(alo.astype(BF16), aup_ref[...]))
    gate = _dot(jax.nn.sigmoid(glo).astype(BF16), gup_ref[...])
    kk_all = k * kk_ref[...]
    k = k * (1.0 + (a - 1.0) * ka_ref[...])

    if valid is not None:
        logw = jnp.where(valid, logw, 0.0)
    cum = jnp.dot(tri, logw, preferred_element_type=F32, precision=lax.Precision.HIGHEST)
    dec_incl, dec_excl, dec_inv = jnp.exp(cum), jnp.exp(cum - logw), jnp.exp(-cum)
    dec_all = dec_incl[chunk - 1:chunk, :]

    out = []
    for h in range(RWKV_HEADS):
        hl = slice(h * HEAD_DIM, (h + 1) * HEAD_DIM)
        kk = kk_all[:, hl]
        kk = kk / jnp.maximum(jnp.sqrt(jnp.sum(kk * kk, axis=-1, keepdims=True)), 1e-12)
        a_vec, b_vec, k_h = -kk, kk * a[:, hl], k[:, hl]
        if valid is not None:
            a_vec = jnp.where(valid, a_vec, 0.0)
            b_vec = jnp.where(valid, b_vec, 0.0)
            k_h = jnp.where(valid, k_h, 0.0)
        r_h, v_h = r[:, hl], v[:, hl]
        s0 = s_ref[bi, h]
        out.append(dict(
            a_t=(a_vec * dec_excl[:, hl]).astype(BF16), r_t=(r_h * dec_incl[:, hl]).astype(BF16),
            b_t=(b_vec * dec_inv[:, hl]).astype(BF16), k_t=(k_h * dec_inv[:, hl]).astype(BF16),
            v_b=v_h.astype(BF16), v_h=v_h, r_h=r_h, k_h=k_h, s0=s0, s0_b=s0.astype(BF16),
            dec_all=dec_all[:, hl], gate=gate[:, hl]))
    return out


_RWKV_PARAMS = ('shift_mix', 'w0', 'w_up', 'a0', 'a_up', 'g_up', 'k_k', 'k_a', 'r_k', 'lnx_g', 'lnx_b')


def _rwkv(l, cols_c, prev, s0, params, t_valid=None):
    bsz, t, _ = cols_c.shape
    chunk = _row_tile(t, RWKV_CHUNK)
    nb = _row_tile(bsz, RWKV_BATCH)
    state = pl.BlockSpec((nb, RWKV_HEADS, HEAD_DIM, HEAD_DIM), lambda b, c: (b, 0, 0, 0))
    return pl.pallas_call(
        functools.partial(_rwkv_body, t_valid=t_valid), grid=(bsz // nb, t // chunk),
        in_specs=[pl.BlockSpec((nb, chunk, C_COLS), lambda b, c: (b, c, 0)),
                  pl.BlockSpec((nb, 1, C_COLS), lambda b, c: (b, 0, 0)), state]
                 + [_layer(a, l) for a in params],
        out_specs=[pl.BlockSpec((nb, chunk, RWKV_W), lambda b, c: (b, c, 0)), state],
        out_shape=[jax.ShapeDtypeStruct((bsz, t, RWKV_W), F32),
                   jax.ShapeDtypeStruct(s0.shape, F32)],
        scratch_shapes=[pltpu.VMEM((nb, C_COLS), F32)],
        compiler_params=_cparams("parallel", "arbitrary"), name="rwkv",
    )(cols_c, prev, s0, *params)


def _xattn_body(x_ref, g_ref, wq_ref, wo_ref, mk_ref, mv_ref, o_ref):
    x = x_ref[0]
    h = _rms(x, g_ref[...]).astype(BF16)
    q = _dot(h, wq_ref[...]) * (XATTN_HEAD_DIM ** -0.5)
    outs = []
    for hh in range(XATTN_HEADS):
        hl = slice(hh * XATTN_HEAD_DIM, (hh + 1) * XATTN_HEAD_DIM)
        s = _dot_nt(q[:, hl].astype(BF16), mk_ref[0, :, hl].astype(BF16))
        m = s.max(axis=-1, keepdims=True)
        p = jnp.exp(s - m)
        den = p.sum(axis=-1, keepdims=True)
        outs.append(_dot(p.astype(BF16), mv_ref[0, :, hl].astype(BF16)) / den)
    o = jnp.concatenate(outs, axis=1).astype(BF16)
    o_ref[0] = x + _dot(o, wo_ref[...])


def _xattn(l, x, g, wq, wo, mk, mv):
    bsz, t, d = x.shape
    tq = _row_tile(t, ROW_TILE)
    nm = mk.shape[1]
    row = pl.BlockSpec((1, tq, d), lambda b, i: (b, i, 0))
    mem = pl.BlockSpec((1, nm, d), lambda b, i: (b, 0, 0))
    return pl.pallas_call(
        _xattn_body, grid=(bsz, t // tq),
        in_specs=[row, _layer(g, l), _layer(wq, l), _layer(wo, l), mem, mem],
        out_specs=row, out_shape=jax.ShapeDtypeStruct(x.shape, F32),
        compiler_params=_cparams("parallel", "parallel"), name="xattn",
    )(x, g, wq, wo, mk, mv)


def _final_norm_body(x_ref, g_ref, o_ref):
    o_ref[...] = _rms(x_ref[...], g_ref[...])


def _final_norm(x, g):
    m, d = x.shape
    tm = _row_tile(m, 2 * ROW_TILE)
    row = pl.BlockSpec((tm, d), lambda i: (i, 0))
    return pl.pallas_call(
        _final_norm_body, grid=(m // tm,), in_specs=[row, _resident((1, d))], out_specs=row,
        out_shape=jax.ShapeDtypeStruct((m, d), F32), compiler_params=_cparams("parallel"),
        name="final_norm",
    )(x, g.reshape(1, d))


def _pad_rows(a, rows):
    return jnp.pad(a, ((0, 0), (0, rows - a.shape[1]), (0, 0)))


def _window_layout(c):
    d, b, n, h, e = c.shape
    return jnp.transpose(c, (0, 1, 3, 4, 2)).reshape(d, b, h * e, n)


def _window_unlayout(c):
    d, b, he, n = c.shape
    return jnp.transpose(c.reshape(d, b, he // HEAD_DIM, HEAD_DIM, n), (0, 1, 4, 2, 3))


def _trunk(x, pos, P, mem, mem_k_cache, mem_v_cache, conv_cache, win_caches, shift_cache, wkv_cache):
    prompt = mem is not None
    bsz, t, d = x.shape
    ng = N_GROUPS
    tp = t if prompt else -(-t // SUBLANES) * SUBLANES
    cos_t, sin_t = _rope_tables(pos if prompt else pos[0] + jnp.arange(tp, dtype=pos.dtype))
    rwkv_params = [P['rwkv_' + n] for n in _RWKV_PARAMS]
    if not prompt:
        caches = [_window_layout(win_caches[g][j]) for j in range(2) for g in range(ng)]
    x = x.reshape(bsz * t, d)
    conv_new, shift_new, wkv_new, memk_new, memv_new = [], [], [], [], []
    win_new = None
    for l in range(DEPTH):
        x = _ffn(l, x, P['ffn1_norm'], P['ffn1_w_gate'], P['ffn1_w_up'], P['ffn1_w_down'])

        cols = _norm_proj(l, x, P['mix_norm'], P['w_in'], (A_COLS, 2 * ATTN_W) + (GROUP_W,) * ng + (C_COLS,))
        cols = [c.reshape(bsz, t, c.shape[1]) for c in cols]
        cols_a, qk, vs, cols_c = cols[0], cols[1], cols[2:2 + ng], cols[-1]

        hist = jnp.zeros((bsz, CONV_HIST, CONV_CH), F32) if prompt else conv_cache[l]
        y_a, conv_state = _conv_mixer(l, cols_a, hist, P['conv_w'], P['conv_b'], P['conv_ln_g'], P['conv_ln_b'])
        conv_new.append(conv_state)

        if prompt:
            y_bs, win_new = _attn_prompt(l, qk, vs, cos_t, sin_t, win_new)
            y_bs = [y.reshape(bsz * t, GROUP_W) for y in y_bs]
        else:
            y_b, win_new = _attn_decode(l, _pad_rows(qk, tp), [_pad_rows(v, tp) for v in vs],
                                        cos_t, sin_t, caches, win_new, t)
            y_bs = [y_b[:, :t].reshape(bsz * t, ATTN_W)]

        if prompt:
            prev = jnp.zeros((bsz, 1, C_COLS), F32)
            s0 = jnp.zeros((bsz, RWKV_HEADS, HEAD_DIM, HEAD_DIM), F32)
            y_c, s_new = _rwkv(l, cols_c, prev, s0, rwkv_params)
        else:
            y_c, s_new = _rwkv(l, _pad_rows(cols_c, tp), shift_cache[l][:, None, :], wkv_cache[l], rwkv_params,
                               t_valid=None if tp == t else t)
            y_c = y_c[:, :t]
        shift_new.append(cols_c[:, -1])
        wkv_new.append(s_new)

        x = _out_proj(l, x, [y_a.reshape(bsz * t, CONV_CH)] + y_bs + [y_c.reshape(bsz * t, RWKV_W)], P['w_out'])

        if prompt:
            nm = mem.shape[1]
            mk, mv = _norm_proj(l, mem.reshape(bsz * nm, d), P['mem_norm'], P['xattn_w_kv'], (d, d))
            mk, mv = mk.reshape(bsz, nm, d), mv.reshape(bsz, nm, d)
            memk_new.append(mk.reshape(bsz, nm, XATTN_HEADS, XATTN_HEAD_DIM))
            memv_new.append(mv.reshape(bsz, nm, XATTN_HEADS, XATTN_HEAD_DIM))
        else:
            nm = mem_k_cache.shape[2]
            mk = mem_k_cache[l].reshape(bsz, nm, d)
            mv = mem_v_cache[l].reshape(bsz, nm, d)
        x = _xattn(l, x.reshape(bsz, t, d), P['xattn_norm'], P['xattn_w_q'], P['xattn_w_o'],
                   mk, mv).reshape(bsz * t, d)

        x = _ffn(l, x, P['ffn2_norm'], P['ffn2_w_gate'], P['ffn2_w_up'], P['ffn2_w_down'])

    y = _final_norm(x, P['final_norm']).reshape(bsz, t, d)
    stack = jnp.stack
    outs = [y, stack(conv_new)]
    for g in range(ng):
        outs += [_window_unlayout(win_new[g]), _window_unlayout(win_new[ng + g])]
    outs += [stack(shift_new), stack(wkv_new)]
    if prompt:
        outs += [stack(memk_new), stack(memv_new)]
    return outs


_MATMUL_WEIGHTS = ('ffn1_w_gate', 'ffn1_w_up', 'ffn1_w_down', 'w_in', 'w_out', 'rwkv_w_up', 'rwkv_a_up',
                   'rwkv_g_up', 'xattn_w_q', 'xattn_w_kv', 'xattn_w_o', 'ffn2_w_gate', 'ffn2_w_up',
                   'ffn2_w_down')
_ROW_VECTORS = ('ffn1_norm', 'mix_norm', 'conv_b', 'conv_ln_g', 'conv_ln_b', 'rwkv_shift_mix', 'rwkv_w0',
                'rwkv_a0', 'rwkv_k_k', 'rwkv_k_a', 'rwkv_r_k', 'rwkv_lnx_g', 'rwkv_lnx_b', 'xattn_norm',
                'mem_norm', 'ffn2_norm')


def kernel(x_prompt, x_sample, cache_conv, cache_win1_k, cache_win1_v, cache_win2_k, cache_win2_v,
           cache_win3_k, cache_win3_v, state_shift, state_wkv, cache_mem_k, cache_mem_v, mem_prompt,
           ffn1_norm, ffn1_w_gate, ffn1_w_up, ffn1_w_down, mix_norm, w_in, w_out,
           conv_w, conv_b, conv_ln_g, conv_ln_b,
           rwkv_shift_mix, rwkv_w0, rwkv_w_up, rwkv_a0, rwkv_a_up, rwkv_g_up, rwkv_k_k, rwkv_k_a,
           rwkv_r_k, rwkv_lnx_g, rwkv_lnx_b,
           xattn_norm, mem_norm, xattn_w_q, xattn_w_kv, xattn_w_o,
           ffn2_norm, ffn2_w_gate, ffn2_w_up, ffn2_w_down, final_norm):
    P = dict(ffn1_norm=ffn1_norm, ffn1_w_gate=ffn1_w_gate, ffn1_w_up=ffn1_w_up, ffn1_w_down=ffn1_w_down,
             mix_norm=mix_norm, w_in=w_in, w_out=w_out,
             conv_w=conv_w, conv_b=conv_b, conv_ln_g=conv_ln_g, conv_ln_b=conv_ln_b,
             rwkv_shift_mix=rwkv_shift_mix, rwkv_w0=rwkv_w0, rwkv_w_up=rwkv_w_up, rwkv_a0=rwkv_a0,
             rwkv_a_up=rwkv_a_up, rwkv_g_up=rwkv_g_up, rwkv_k_k=rwkv_k_k, rwkv_k_a=rwkv_k_a,
             rwkv_r_k=rwkv_r_k, rwkv_lnx_g=rwkv_lnx_g, rwkv_lnx_b=rwkv_lnx_b,
             xattn_norm=xattn_norm, mem_norm=mem_norm, xattn_w_q=xattn_w_q, xattn_w_kv=xattn_w_kv,
             xattn_w_o=xattn_w_o, ffn2_norm=ffn2_norm, ffn2_w_gate=ffn2_w_gate, ffn2_w_up=ffn2_w_up,
             ffn2_w_down=ffn2_w_down, final_norm=final_norm)
    for name in _MATMUL_WEIGHTS:
        P[name] = P[name].astype(BF16)
    for name in _ROW_VECTORS:
        P[name] = P[name].reshape(DEPTH, 1, -1)

    pos_p = jnp.arange(x_prompt.shape[1], dtype=jnp.int32)
    outs_p = _trunk(x_prompt, pos_p, P, mem_prompt, None, None, None, None, None, None)

    pos_s = PAST_LEN + jnp.arange(x_sample.shape[1], dtype=jnp.int32)
    win_caches = [(cache_win1_k, cache_win1_v), (cache_win2_k, cache_win2_v), (cache_win3_k, cache_win3_v)]
    outs_s = _trunk(x_sample, pos_s, P, None, cache_mem_k, cache_mem_v, cache_conv, win_caches,
                    state_shift, state_wkv)

    return (outs_p[0], outs_s[0], *outs_p[1:], *outs_s[1:])
```

```python
import functools

import jax
import jax.numpy as jnp
from jax import lax
from jax.experimental import pallas as pl
from jax.experimental.pallas import tpu as pltpu

F32 = jnp.float32
BF16 = jnp.bfloat16

D_MODEL = 1024
DEPTH = 4
HEAD_DIM = 64
CONV_CH = 256
CONV_WIDTH = 31
ATTN_GROUPS = ((128, 1), (512, 4), (2048, 16))
N_GROUPS = len(ATTN_GROUPS)
GROUP_W = 2 * HEAD_DIM
ATTN_W = GROUP_W * N_GROUPS
ATTN_SCALE = HEAD_DIM ** -0.5
ATTN_BLOCK = 128
ATTN_UNROLL = 4
RWKV_HEADS = 6
RWKV_W = RWKV_HEADS * HEAD_DIM
LORA_DECAY = 64
LORA_AAA = 64
LORA_GATE = 128
LNX_EPS = 64e-5
ROPE_THETA = 10000.0
XATTN_HEADS = 4
XATTN_HEAD_DIM = D_MODEL // XATTN_HEADS
NORM_EPS = 1e-6
A_COLS = 2 * CONV_CH
B_COLS = 3 * ATTN_W
C_COLS = 3 * RWKV_W + LORA_DECAY + LORA_AAA + LORA_GATE
PAST_LEN = 8192

VMEM_LIMIT_BYTES = 56 * 1024 * 1024
LANES = 128
SUBLANES = 8
ROW_TILE = 512
FFN_TF = 256
RWKV_CHUNK = 64
RWKV_BATCH = 4


def _cparams(*sem):
    return pltpu.CompilerParams(dimension_semantics=sem, vmem_limit_bytes=VMEM_LIMIT_BYTES)


def _resident(shape):
    nd = len(shape)
    return pl.BlockSpec(shape, lambda *_: (0,) * nd, pipeline_mode=pl.Buffered(1))


def _layer(arr, l):
    nd = arr.ndim
    return pl.BlockSpec((None,) + arr.shape[1:], lambda *_: (l,) + (0,) * (nd - 1),
                        pipeline_mode=pl.Buffered(1))


def _carried(carry, shape, n_in):
    if carry is None:
        return [], [], {}
    assert carry.shape == shape
    return [carry], [pl.BlockSpec(memory_space=pl.ANY)], {n_in: None}


def _rms(x, g):
    return x * lax.rsqrt(jnp.mean(x * x, axis=-1, keepdims=True) + NORM_EPS) * g


def _dot(a, b):
    return jnp.dot(a, b, preferred_element_type=F32)


def _dot_nt(a, b):
    return lax.dot_general(a, b, (((1,), (1,)), ((), ())), preferred_element_type=F32)


def _row_tile(m, target):
    t = min(m, target)
    assert m % t == 0, (m, t)
    return t


def _ffn_body(x_ref, g_ref, wg_ref, wu_ref, wd_ref, o_ref):
    x = x_ref[...]
    h = _rms(x, g_ref[...]).astype(BF16)
    acc = jnp.zeros(x.shape, F32)
    for j in range(wg_ref.shape[1] // FFN_TF):
        sl = slice(j * FFN_TF, (j + 1) * FFN_TF)
        gate = _dot(h, wg_ref[:, sl])
        up = _dot(h, wu_ref[:, sl])
        act = (gate * jax.nn.sigmoid(gate) * up).astype(BF16)
        acc = acc + _dot(act, wd_ref[sl, :])
    o_ref[...] = x + 0.5 * acc


def _ffn(l, x, g, wg, wu, wd):
    m, d = x.shape
    tm = _row_tile(m, ROW_TILE)
    row = pl.BlockSpec((tm, d), lambda i: (i, 0))
    return pl.pallas_call(
        _ffn_body, grid=(m // tm,),
        in_specs=[row, _layer(g, l), _layer(wg, l), _layer(wu, l), _layer(wd, l)],
        out_specs=row, out_shape=jax.ShapeDtypeStruct((m, d), F32),
        compiler_params=_cparams("parallel"), name="ffn",
    )(x, g, wg, wu, wd)


def _norm_proj_body(x_ref, g_ref, w_ref, *o_refs):
    h = _rms(x_ref[...], g_ref[...]).astype(BF16)
    off = 0
    for o_ref in o_refs:
        n = o_ref.shape[1]
        o_ref[...] = _dot(h, w_ref[:, off:off + n])
        off += n


def _norm_proj(l, x, g, w, splits):
    m, d = x.shape
    tm = _row_tile(m, ROW_TILE)
    return pl.pallas_call(
        _norm_proj_body, grid=(m // tm,),
        in_specs=[pl.BlockSpec((tm, d), lambda i: (i, 0)), _layer(g, l), _layer(w, l)],
        out_specs=[pl.BlockSpec((tm, n), lambda i: (i, 0)) for n in splits],
        out_shape=[jax.ShapeDtypeStruct((m, n), F32) for n in splits],
        compiler_params=_cparams("parallel"), name="norm_proj",
    )(x, g, w)


def _out_proj_body(x_ref, *refs):
    y_refs, w_ref, o_ref = refs[:-2], refs[-2], refs[-1]
    acc = x_ref[...]
    off = 0
    for y_ref in y_refs:
        n = y_ref.shape[1]
        acc = acc + _dot(y_ref[...].astype(BF16), w_ref[off:off + n, :])
        off += n
    o_ref[...] = acc


def _out_proj(l, x, ys, w):
    m, d = x.shape
    tm = _row_tile(m, ROW_TILE)
    row = lambda n: pl.BlockSpec((tm, n), lambda i: (i, 0))
    return pl.pallas_call(
        _out_proj_body, grid=(m // tm,),
        in_specs=[row(d)] + [row(y.shape[1]) for y in ys] + [_layer(w, l)],
        out_specs=row(d), out_shape=jax.ShapeDtypeStruct((m, d), F32),
        compiler_params=_cparams("parallel"), name="out_proj",
    )(x, *ys, w)


CONV_HIST = CONV_WIDTH - 1
CONV_PAD = 32


def _conv_body(ca_ref, hist_ref, w_ref, b_ref, lg_ref, lb_ref, y_ref, st_ref, u_scr, *, chunk):
    t = ca_ref.shape[1]
    x = ca_ref[0]
    u_scr[CONV_PAD - CONV_HIST:CONV_PAD, :] = hist_ref[0]
    u_scr[CONV_PAD:CONV_PAD + t, :] = x[:, :CONV_CH] * jax.nn.sigmoid(x[:, CONV_CH:])
    st_ref[0] = u_scr[CONV_PAD + t - CONV_HIST:CONV_PAD + t, :]
    base = CONV_PAD - CONV_HIST
    for c in range(t // chunk):
        acc = jnp.zeros((chunk, CONV_CH), F32) + b_ref[...]
        for j in range(CONV_WIDTH):
            r0 = c * chunk + base + j
            acc = acc + u_scr[r0:r0 + chunk, :] * w_ref[j:j + 1, :]
        mu = jnp.mean(acc, axis=-1, keepdims=True)
        var = jnp.mean(jnp.square(acc - mu), axis=-1, keepdims=True)
        yn = (acc - mu) * lax.rsqrt(var + 1e-5) * lg_ref[...] + lb_ref[...]
        y_ref[0, c * chunk:(c + 1) * chunk, :] = yn * jax.nn.sigmoid(yn)


def _conv_mixer(l, cols_a, hist, w, b, lg, lb):
    bsz, t, _ = cols_a.shape
    chunk = _row_tile(t, 128)
    return pl.pallas_call(
        functools.partial(_conv_body, chunk=chunk), grid=(bsz,),
        in_specs=[pl.BlockSpec((1, t, A_COLS), lambda i: (i, 0, 0)),
                  pl.BlockSpec((1, CONV_HIST, CONV_CH), lambda i: (i, 0, 0)),
                  _layer(w, l), _layer(b, l), _layer(lg, l), _layer(lb, l)],
        out_specs=[pl.BlockSpec((1, t, CONV_CH), lambda i: (i, 0, 0)),
                   pl.BlockSpec((1, CONV_HIST, CONV_CH), lambda i: (i, 0, 0))],
        out_shape=[jax.ShapeDtypeStruct((bsz, t, CONV_CH), F32),
                   jax.ShapeDtypeStruct((bsz, CONV_HIST, CONV_CH), F32)],
        scratch_shapes=[pltpu.VMEM((CONV_PAD + t, CONV_CH), F32)],
        compiler_params=_cparams("parallel"), name="conv_mixer",
    )(cols_a, hist, w, b, lg, lb)


def _rope_tables(pos):
    half = HEAD_DIM // 2
    inv = jnp.power(ROPE_THETA, -jnp.arange(half, dtype=F32) / half)
    ang = pos.astype(F32)[:, None] * inv[None, :]
    cos, sin = jnp.cos(ang), jnp.sin(ang)
    cos_t = jnp.concatenate([cos, cos], axis=-1)
    sin_t = jnp.concatenate([-sin, sin], axis=-1)
    reps = GROUP_W // HEAD_DIM
    return jnp.tile(cos_t, (1, reps)), jnp.tile(sin_t, (1, reps))


def _rope(x, cos, sin):
    half = HEAD_DIM // 2
    lane = lax.broadcasted_iota(jnp.int32, x.shape, 1)
    first_half = (lane & (HEAD_DIM - 1)) < half
    partner = jnp.where(first_half, pltpu.roll(x, GROUP_W - half, 1), pltpu.roll(x, half, 1))
    return x * cos + partner * sin


def _softmax_parts(scores):
    m = scores[0].max(axis=-1, keepdims=True)
    for s in scores[1:]:
        m = jnp.maximum(m, s.max(axis=-1, keepdims=True))
    ps = [jnp.exp(s - m) for s in scores]
    den = ps[0].sum(axis=-1, keepdims=True)
    for p in ps[1:]:
        den = den + p.sum(axis=-1, keepdims=True)
    return m, ps, den


def _lanes_per_head(vals):
    return jnp.concatenate([jnp.broadcast_to(v, (v.shape[0], HEAD_DIM)) for v in vals], axis=1)


def _mix_dilations(parts):
    m_all = functools.reduce(jnp.maximum, [p[1] for p in parts])
    scl = [jnp.exp(p[1] - m_all) for p in parts]
    tot = functools.reduce(lambda a, b: a + b, [p[2] * s for p, s in zip(parts, scl)])
    return [p[0] * (s / tot) for p, s in zip(parts, scl)]


def _attn_prompt_body(*refs):
    ng = N_GROUPS
    qk_ref, v_refs, cos_ref, sin_ref = refs[0], refs[1:1 + ng], refs[1 + ng], refs[2 + ng]
    rest = refs[3 + ng:]
    n_carry = len(rest) - (3 * ng + 9)
    rest = rest[n_carry:]
    y_refs, wk_refs, wv_refs = rest[:ng], rest[ng:2 * ng], rest[2 * ng:3 * ng]
    q_scr, kr_scr, pq_scr, pk_scr, pv_scr, po_scr, pm_scr, pd_scr, md_scr = rest[3 * ng:]
    t = qk_ref.shape[1]
    rc = min(t, 256)
    nblk = t // ATTN_BLOCK

    def rope_chunk(i, carry):
        rows = pl.ds(pl.multiple_of(i * rc, rc), rc)
        cs, sn = cos_ref[rows, :], sin_ref[rows, :]
        for g in range(ng):
            q_scr[g, rows, :] = _rope(qk_ref[0, rows, g * GROUP_W:(g + 1) * GROUP_W], cs, sn) * ATTN_SCALE
            kr_scr[g, rows, :] = _rope(qk_ref[0, rows, ATTN_W + g * GROUP_W:ATTN_W + (g + 1) * GROUP_W], cs, sn)
        return carry

    lax.fori_loop(0, t // rc, rope_chunk, 0)

    qi = lax.broadcasted_iota(jnp.int32, (ATTN_BLOCK, ATTN_BLOCK), 0)
    kj = lax.broadcasted_iota(jnp.int32, (ATTN_BLOCK, ATTN_BLOCK), 1)

    for g, (win, dil) in enumerate(ATTN_GROUPS):
        assert win // dil == ATTN_BLOCK and t % (dil * ATTN_BLOCK) == 0
        sub_len = t // dil
        sub_blocks = sub_len // ATTN_BLOCK

        keep = wk_refs[g].shape[-1]
        for j in range(keep // LANES):
            src = slice(t - keep + j * LANES, t - keep + (j + 1) * LANES)
            dst = slice(j * LANES, (j + 1) * LANES)
            wk_refs[g][:, dst] = kr_scr[g, src, :].T
            wv_refs[g][:, dst] = v_refs[g][0, src, :].T

        for r in range(dil):
            rows = pl.ds(r, sub_len, stride=dil) if dil > 1 else slice(0, t)
            dst = slice(r * sub_len, (r + 1) * sub_len)
            pq_scr[dst, :] = q_scr[g, rows, :].astype(BF16)
            pk_scr[dst, :] = kr_scr[g, rows, :].astype(BF16)
            pv_scr[dst, :] = v_refs[g][0, rows, :].astype(BF16)

        nh = GROUP_W // HEAD_DIM
        add = lambda a, b: a + b

        def block_group(i, carry):
            owns, chains = [], []
            for u in range(ATTN_UNROLL):
                b = i * ATTN_UNROLL + u
                own = pl.ds(pl.multiple_of(b * ATTN_BLOCK, ATTN_BLOCK), ATTN_BLOCK)
                owns.append(own)
                q, k_own, v_own = pq_scr[own, :], pk_scr[own, :], pv_scr[own, :]
                masks = [kj <= qi]
                if sub_blocks > 1:
                    prev = pl.ds(pl.multiple_of(jnp.maximum(b - 1, 0) * ATTN_BLOCK, ATTN_BLOCK), ATTN_BLOCK)
                    k_prev, v_prev = pk_scr[prev, :], pv_scr[prev, :]
                    masks.append(jnp.logical_and(kj >= qi, lax.rem(b, sub_blocks) > 0))
                for h in range(nh):
                    hl = slice(h * HEAD_DIM, (h + 1) * HEAD_DIM)
                    ks, vs = [k_own[:, hl]], [v_own[:, hl]]
                    if sub_blocks > 1:
                        ks.append(k_prev[:, hl])
                        vs.append(v_prev[:, hl])
                    chains.append((q[:, hl], ks, vs, masks))
            scores = [[jnp.where(mk, _dot_nt(q, kk), -jnp.inf) for mk, kk in zip(masks, ks)]
                      for q, ks, _, masks in chains]
            ms = [functools.reduce(jnp.maximum, [s.max(axis=-1, keepdims=True) for s in sc]) for sc in scores]
            ps = [[jnp.exp(s - m) for s in sc] for sc, m in zip(scores, ms)]
            ds = [functools.reduce(add, [p.sum(axis=-1, keepdims=True) for p in pp]) for pp in ps]
            os = [functools.reduce(add, [_dot(p.astype(BF16), vv) for p, vv in zip(pp, c[2])])
                  for pp, c in zip(ps, chains)]
            for u, own in enumerate(owns):
                sl = slice(u * nh, (u + 1) * nh)
                po_scr[own, :] = jnp.concatenate(os[sl], axis=1)
                pm_scr[own, :] = _lanes_per_head(ms[sl])
                pd_scr[own, :] = _lanes_per_head(ds[sl])
            return carry

        assert nblk % ATTN_UNROLL == 0
        lax.fori_loop(0, nblk // ATTN_UNROLL, block_group, 0)

        for r in range(dil):
            rows = pl.ds(r, sub_len, stride=dil) if dil > 1 else slice(0, t)
            src = slice(r * sub_len, (r + 1) * sub_len)
            y_refs[g][0, rows, :] = po_scr[src, :]
            md_scr[g, rows, :] = pm_scr[src, :]
            md_scr[ng + g, rows, :] = pd_scr[src, :]

    def combine_chunk(i, carry):
        rows = pl.ds(pl.multiple_of(i * rc, rc), rc)
        mixed = _mix_dilations([(y_refs[g][0, rows, :], md_scr[g, rows, :], md_scr[ng + g, rows, :])
                                for g in range(ng)])
        for g in range(ng):
            y_refs[g][0, rows, :] = mixed[g]
        return carry

    lax.fori_loop(0, t // rc, combine_chunk, 0)


def _attn_prompt(l, qk, vs, cos_t, sin_t, carries):
    bsz, t, _ = qk.shape
    ng = N_GROUPS
    grp = pl.BlockSpec((1, t, GROUP_W), lambda i: (i, 0, 0))
    grp_in = pl.BlockSpec((1, t, GROUP_W), lambda i: (i, 0, 0), pipeline_mode=pl.Buffered(1))
    keeps = [min(win, t) for win, _ in ATTN_GROUPS]
    win_shapes = [(DEPTH, bsz, GROUP_W, kp) for kp in keeps] * 2
    win_specs = [pl.BlockSpec((None, None, GROUP_W, kp), lambda i: (l, i, 0, 0)) for kp in keeps] * 2
    n_in = 3 + ng
    c_in, c_specs, aliases = [], [], {}
    for j, shp in enumerate(win_shapes):
        a, s, al = _carried(None if carries is None else carries[j], shp, n_in + j)
        c_in, c_specs = c_in + a, c_specs + s
        aliases.update({k: ng + j for k in al})
    outs = pl.pallas_call(
        _attn_prompt_body, grid=(bsz,),
        in_specs=[pl.BlockSpec((1, t, 2 * ATTN_W), lambda i: (i, 0, 0), pipeline_mode=pl.Buffered(1))]
                 + [grp_in] * ng + [_resident((t, GROUP_W)), _resident((t, GROUP_W))] + c_specs,
        out_specs=[grp] * ng + win_specs,
        out_shape=[jax.ShapeDtypeStruct((bsz, t, GROUP_W), F32)] * ng
                  + [jax.ShapeDtypeStruct(s, F32) for s in win_shapes],
        scratch_shapes=[pltpu.VMEM((ng, t, GROUP_W), F32), pltpu.VMEM((ng, t, GROUP_W), F32),
                        pltpu.VMEM((t, GROUP_W), BF16), pltpu.VMEM((t, GROUP_W), BF16),
                        pltpu.VMEM((t, GROUP_W), BF16),
                        pltpu.VMEM((t, GROUP_W), F32), pltpu.VMEM((t, GROUP_W), F32),
                        pltpu.VMEM((t, GROUP_W), F32),
                        pltpu.VMEM((2 * ng, t, GROUP_W), F32)],
        input_output_aliases=aliases,
        compiler_params=_cparams("parallel"), name="attn_prompt",
    )(qk, *vs, cos_t, sin_t, *c_in)
    return outs[:ng], outs[ng:]


def _attn_decode_body(*refs, t_new):
    ng = N_GROUPS
    qk_ref, v_refs, cos_ref, sin_ref = refs[0], refs[1:1 + ng], refs[1 + ng], refs[2 + ng]
    cache_refs = refs[3 + ng:3 + 3 * ng]
    rest = refs[3 + 3 * ng:]
    n_carry = len(rest) - (1 + 2 * ng + 1)
    rest = rest[n_carry:]
    y_ref, out_refs, pad_scr = rest[0], rest[1:1 + 2 * ng], rest[-1]
    tp = qk_ref.shape[1]
    cs, sn = cos_ref[...], sin_ref[...]
    tail_lane = lax.broadcasted_iota(jnp.int32, (GROUP_W, LANES), 1)
    parts = []
    for g, (win, dil) in enumerate(ATTN_GROUPS):
        assert dil & (dil - 1) == 0
        n_old = cache_refs[g].shape[-1]
        q = (_rope(qk_ref[0, :, g * GROUP_W:(g + 1) * GROUP_W], cs, sn) * ATTN_SCALE).astype(BF16)
        k_new = _rope(qk_ref[0, :, ATTN_W + g * GROUP_W:ATTN_W + (g + 1) * GROUP_W], cs, sn)
        v_new = v_refs[g][0]
        old = []
        for new_rows, c_ref, o_ref in ((k_new, cache_refs[g], out_refs[g]),
                                       (v_new, cache_refs[ng + g], out_refs[ng + g])):
            c_old = c_ref[...]
            moved = pltpu.roll(c_old, n_old - t_new, 1)
            pad_scr[...] = jnp.zeros(pad_scr.shape, F32)
            pad_scr[LANES - t_new:LANES, :] = new_rows[0:t_new, :]
            tail = jnp.where(tail_lane >= LANES - t_new, pad_scr[...].T, moved[:, n_old - LANES:])
            if n_old > LANES:
                o_ref[:, 0:n_old - LANES] = moved[:, 0:n_old - LANES]
            o_ref[:, n_old - LANES:] = tail
            old.append(c_old.astype(BF16))
        k_old, v_old = old
        dist_old = (n_old + lax.broadcasted_iota(jnp.int32, (tp, n_old), 0)
                    - lax.broadcasted_iota(jnp.int32, (tp, n_old), 1))
        ok_old = jnp.logical_and(dist_old <= win, (dist_old & (dil - 1)) == 0)
        dist_new = (lax.broadcasted_iota(jnp.int32, (tp, tp), 0) - lax.broadcasted_iota(jnp.int32, (tp, tp), 1))
        ok_new = jnp.logical_and(jnp.logical_and(dist_new >= 0, dist_new <= win), (dist_new & (dil - 1)) == 0)
        os, ms, ds = [], [], []
        for h in range(GROUP_W // HEAD_DIM):
            hl = slice(h * HEAD_DIM, (h + 1) * HEAD_DIM)
            s_old = jnp.where(ok_old, _dot(q[:, hl], k_old[hl, :]), -jnp.inf)
            s_new = jnp.where(ok_new, _dot_nt(q[:, hl], k_new[:, hl].astype(BF16)), -jnp.inf)
            m, (p_old, p_new), den = _softmax_parts([s_old, s_new])
            os.append(_dot_nt(p_old.astype(BF16), v_old[hl, :]) + _dot(p_new.astype(BF16), v_new[:, hl].astype(BF16)))
            ms.append(m)
            ds.append(den)
        parts.append((jnp.concatenate(os, axis=1), _lanes_per_head(ms), _lanes_per_head(ds)))
    y_ref[0] = jnp.concatenate(_mix_dilations(parts), axis=1)


def _attn_decode(l, qk, vs, cos_t, sin_t, caches, carries, t_new):
    bsz, tp, _ = qk.shape
    ng = N_GROUPS
    cache_specs = [pl.BlockSpec((None, None, GROUP_W, c.shape[-1]), lambda i: (l, i, 0, 0)) for c in caches]
    n_in = 3 + 3 * ng
    c_in, c_specs, aliases = [], [], {}
    for j, c in enumerate(caches):
        a, s, al = _carried(None if carries is None else carries[j], c.shape, n_in + j)
        c_in, c_specs = c_in + a, c_specs + s
        aliases.update({k: 1 + j for k in al})
    outs = pl.pallas_call(
        functools.partial(_attn_decode_body, t_new=t_new), grid=(bsz,),
        in_specs=[pl.BlockSpec((1, tp, 2 * ATTN_W), lambda i: (i, 0, 0))]
                 + [pl.BlockSpec((1, tp, GROUP_W), lambda i: (i, 0, 0))] * ng
                 + [_resident((tp, GROUP_W)), _resident((tp, GROUP_W))] + cache_specs + c_specs,
        out_specs=[pl.BlockSpec((1, tp, ATTN_W), lambda i: (i, 0, 0))] + cache_specs,
        out_shape=[jax.ShapeDtypeStruct((bsz, tp, ATTN_W), F32)]
                  + [jax.ShapeDtypeStruct(c.shape, F32) for c in caches],
        scratch_shapes=[pltpu.VMEM((LANES, GROUP_W), F32)],
        input_output_aliases=aliases,
        compiler_params=_cparams("parallel"), name="attn_decode",
    )(qk, *vs, cos_t, sin_t, *caches, *c_in)
    return outs[0], outs[1:]


def _rwkv_body(cc_ref, prev_ref, s0_ref, mix_ref, w0_ref, wup_ref, a0_ref, aup_ref, gup_ref,
               kk_ref, ka_ref, rk_ref, lng_ref, lnb_ref, y_ref, s_ref, prev_scr, *, t_valid):
    c = pl.program_id(1)
    nb, chunk = cc_ref.shape[0], cc_ref.shape[1]

    @pl.when(c == 0)
    def _():
        s_ref[...] = s0_ref[...]
        for bi in range(nb):
            prev_scr[bi:bi + 1, :] = prev_ref[bi]

    row = lax.broadcasted_iota(jnp.int32, (chunk, 1), 0)
    ri = lax.broadcasted_iota(jnp.int32, (chunk, chunk), 0)
    ci = lax.broadcasted_iota(jnp.int32, (chunk, chunk), 1)
    incl, strict = ri >= ci, ri > ci
    tri = incl.astype(F32)
    masked = t_valid is not None
    if masked:
        valid = (c * chunk + row) < t_valid

    probs = []
    for bi in range(nb):
        probs += _rwkv_prepare(bi, cc_ref, prev_scr, s_ref, mix_ref, w0_ref, wup_ref, a0_ref, aup_ref, gup_ref,
                               kk_ref, ka_ref, tri, row, valid if masked else None)

    bf = lambda xs_: [x.astype(BF16) for x in xs_]
    a_ab = bf([jnp.where(strict, _dot_nt(p['a_t'], p['b_t']), 0.0) for p in probs])
    a_ak = bf([jnp.where(strict, _dot_nt(p['a_t'], p['k_t']), 0.0) for p in probs])
    a_rb = bf([jnp.where(incl, _dot_nt(p['r_t'], p['b_t']), 0.0) for p in probs])
    a_rk = bf([jnp.where(incl, _dot_nt(p['r_t'], p['k_t']), 0.0) for p in probs])
    u = [_dot_nt(p['a_t'], p['s0_b']) + _dot(m, p['v_b']) for p, m in zip(probs, a_ak)]
    o_part = [_dot_nt(p['r_t'], p['s0_b']) + _dot(m, p['v_b']) for p, m in zip(probs, a_rk)]
    s_part = [p['s0'] + _dot(p['v_h'].T.astype(BF16), p['k_t']) for p in probs]
    pw = a_ab
    n_levels = max((chunk - 1).bit_length(), 1)
    for lvl in range(n_levels):
        u_next = [ui + _dot(m, ui.astype(BF16)) for ui, m in zip(u, pw)]
        if lvl + 1 < n_levels:
            pw = bf([_dot(m, m) for m in pw])
        u = u_next
    o = [op + _dot(m, ui.astype(BF16)) for op, m, ui in zip(o_part, a_rb, u)]
    s_new = [(sp + _dot(ui.T.astype(BF16), p['b_t'])) * p['dec_all'] for sp, ui, p in zip(s_part, u, probs)]

    for bi in range(nb):
        ys = []
        for h in range(RWKV_HEADS):
            i = bi * RWKV_HEADS + h
            p, hl = probs[i], slice(h * HEAD_DIM, (h + 1) * HEAD_DIM)
            s_ref[bi, h] = s_new[i]
            mu = jnp.mean(o[i], axis=-1, keepdims=True)
            var = jnp.mean(jnp.square(o[i] - mu), axis=-1, keepdims=True)
            on = (o[i] - mu) * lax.rsqrt(var + LNX_EPS) * lng_ref[:, hl] + lnb_ref[:, hl]
            bonus = jnp.sum(p['r_h'] * p['k_h'] * rk_ref[:, hl], axis=-1, keepdims=True) * p['v_h']
            ys.append((on + bonus) * p['gate'])
        y_ref[bi] = jnp.concatenate(ys, axis=1)


def _rwkv_prepare(bi, cc_ref, prev_scr, s_ref, mix_ref, w0_ref, wup_ref, a0_ref, aup_ref, gup_ref,
                  kk_ref, ka_ref, tri, row, valid):
    chunk = cc_ref.shape[1]
    f = cc_ref[bi]
    shifted = jnp.where(row == 0, prev_scr[bi:bi + 1, :], pltpu.roll(f, 1, 0))
    prev_scr[bi:bi + 1, :] = f[chunk - 1:chunk, :]
    xs = f + (shifted - f) * mix_ref[...]

    r = xs[:, 0:RWKV_W]
    k = xs[:, RWKV_W:2 * RWKV_W]
    v = xs[:, 2 * RWKV_W:3 * RWKV_W]
    o1 = 3 * RWKV_W
    wlo = xs[:, o1:o1 + LORA_DECAY]
    alo = xs[:, o1 + LORA_DECAY:o1 + LORA_DECAY + LORA_AAA]
    glo = xs[:, o1 + LORA_DECAY + LORA_AAA:]
    w = -jax.nn.softplus(-(w0_ref[...] + _dot(jnp.tanh(wlo).astype(BF16), wup_ref[...]))) - 0.5
    logw = -jnp.exp(w)
    a = jax.nn.sigmoid(a0_ref[...] + _dot(alo.astype(BF16), aup_ref[...]))
    gate = _dot(jax.nn.sigmoid(glo).astype(BF16), gup_ref[...])
    kk_all = k * kk_ref[...]
    k = k * (1.0 + (a - 1.0) * ka_ref[...])

    if valid is not None:
        logw = jnp.where(valid, logw, 0.0)
    cum = jnp.dot(tri, logw, preferred_element_type=F32, precision=lax.Precision.HIGHEST)
    dec_incl, dec_excl, dec_inv = jnp.exp(cum), jnp.exp(cum - logw), jnp.exp(-cum)
    dec_all = dec_incl[chunk - 1:chunk, :]

    out = []
    for h in range(RWKV_HEADS):
        hl = slice(h * HEAD_DIM, (h + 1) * HEAD_DIM)
        kk = kk_all[:, hl]
        kk = kk / jnp.maximum(jnp.sqrt(jnp.sum(kk * kk, axis=-1, keepdims=True)), 1e-12)
        a_vec, b_vec, k_h = -kk, kk * a[:, hl], k[:, hl]
        if valid is not None:
            a_vec = jnp.where(valid, a_vec, 0.0)
            b_vec = jnp.where(valid, b_vec, 0.0)
            k_h = jnp.where(valid, k_h, 0.0)
        r_h, v_h = r[:, hl], v[:, hl]
        s0 = s_ref[bi, h]
        out.append(dict(
            a_t=(a_vec * dec_excl[:, hl]).astype(BF16), r_t=(r_h * dec_incl[:, hl]).astype(BF16),
            b_t=(b_vec * dec_inv[:, hl]).astype(BF16), k_t=(k_h * dec_inv[:, hl]).astype(BF16),
            v_b=v_h.astype(BF16), v_h=v_h, r_h=r_h, k_h=k_h, s0=s0, s0_b=s0.astype(BF16),
            dec_all=dec_all[:, hl], gate=gate[:, hl]))
    return out


_RWKV_PARAMS = ('shift_mix', 'w0', 'w_up', 'a0', 'a_up', 'g_up', 'k_k', 'k_a', 'r_k', 'lnx_g', 'lnx_b')


def _rwkv(l, cols_c, prev, s0, params, t_valid=None):
    bsz, t, _ = cols_c.shape
    chunk = _row_tile(t, RWKV_CHUNK)
    nb = _row_tile(bsz, RWKV_BATCH)
    state = pl.BlockSpec((nb, RWKV_HEADS, HEAD_DIM, HEAD_DIM), lambda b, c: (b, 0, 0, 0))
    return pl.pallas_call(
        functools.partial(_rwkv_body, t_valid=t_valid), grid=(bsz // nb, t // chunk),
        in_specs=[pl.BlockSpec((nb, chunk, C_COLS), lambda b, c: (b, c, 0)),
                  pl.BlockSpec((nb, 1, C_COLS), lambda b, c: (b, 0, 0)), state]
                 + [_layer(a, l) for a in params],
        out_specs=[pl.BlockSpec((nb, chunk, RWKV_W), lambda b, c: (b, c, 0)), state],
        out_shape=[jax.ShapeDtypeStruct((bsz, t, RWKV_W), F32),
                   jax.ShapeDtypeStruct(s0.shape, F32)],
        scratch_shapes=[pltpu.VMEM((nb, C_COLS), F32)],
        compiler_params=_cparams("parallel", "arbitrary"), name="rwkv",
    )(cols_c, prev, s0, *params)


def _xattn_body(x_ref, *refs):
    y_refs, (wout_ref, g_ref, wq_ref, wo_ref, mk_ref, mv_ref, o_ref) = refs[:-7], refs[-7:]
    x = x_ref[0]
    off = 0
    for y_ref in y_refs:
        n = y_ref.shape[2]
        x = x + _dot(y_ref[0].astype(BF16), wout_ref[off:off + n, :])
        off += n
    h = _rms(x, g_ref[...]).astype(BF16)
    q = _dot(h, wq_ref[...]) * (XATTN_HEAD_DIM ** -0.5)
    outs = []
    for hh in range(XATTN_HEADS):
        hl = slice(hh * XATTN_HEAD_DIM, (hh + 1) * XATTN_HEAD_DIM)
        s = _dot_nt(q[:, hl].astype(BF16), mk_ref[0, :, hl].astype(BF16))
        m = s.max(axis=-1, keepdims=True)
        p = jnp.exp(s - m)
        den = p.sum(axis=-1, keepdims=True)
        outs.append(_dot(p.astype(BF16), mv_ref[0, :, hl].astype(BF16)) / den)
    o = jnp.concatenate(outs, axis=1).astype(BF16)
    o_ref[0] = x + _dot(o, wo_ref[...])


def _xattn(l, x, ys, w_out, g, wq, wo, mk, mv):
    bsz, t, d = x.shape
    tq = _row_tile(t, ROW_TILE)
    nm = mk.shape[1]
    row = lambda n: pl.BlockSpec((1, tq, n), lambda b, i: (b, i, 0))
    mem = pl.BlockSpec((1, nm, d), lambda b, i: (b, 0, 0))
    return pl.pallas_call(
        _xattn_body, grid=(bsz, t // tq),
        in_specs=[row(d)] + [row(y.shape[2]) for y in ys]
                 + [_layer(w_out, l), _layer(g, l), _layer(wq, l), _layer(wo, l), mem, mem],
        out_specs=row(d), out_shape=jax.ShapeDtypeStruct(x.shape, F32),
        compiler_params=_cparams("parallel", "parallel"), name="xattn",
    )(x, *ys, w_out, g, wq, wo, mk, mv)


def _final_norm_body(x_ref, g_ref, o_ref):
    o_ref[...] = _rms(x_ref[...], g_ref[...])


def _final_norm(x, g):
    m, d = x.shape
    tm = _row_tile(m, 2 * ROW_TILE)
    row = pl.BlockSpec((tm, d), lambda i: (i, 0))
    return pl.pallas_call(
        _final_norm_body, grid=(m // tm,), in_specs=[row, _resident((1, d))], out_specs=row,
        out_shape=jax.ShapeDtypeStruct((m, d), F32), compiler_params=_cparams("parallel"),
        name="final_norm",
    )(x, g.reshape(1, d))


def _pad_rows(a, rows):
    return jnp.pad(a, ((0, 0), (0, rows - a.shape[1]), (0, 0)))


def _window_layout(c):
    d, b, n, h, e = c.shape
    return jnp.transpose(c, (0, 1, 3, 4, 2)).reshape(d, b, h * e, n)


def _window_unlayout(c):
    d, b, he, n = c.shape
    return jnp.transpose(c.reshape(d, b, he // HEAD_DIM, HEAD_DIM, n), (0, 1, 4, 2, 3))


def _trunk(x, pos, P, mem, mem_k_cache, mem_v_cache, conv_cache, win_caches, shift_cache, wkv_cache):
    prompt = mem is not None
    bsz, t, d = x.shape
    ng = N_GROUPS
    tp = t if prompt else -(-t // SUBLANES) * SUBLANES
    cos_t, sin_t = _rope_tables(pos if prompt else pos[0] + jnp.arange(tp, dtype=pos.dtype))
    rwkv_params = [P['rwkv_' + n] for n in _RWKV_PARAMS]
    if not prompt:
        caches = [_window_layout(win_caches[g][j]) for j in range(2) for g in range(ng)]
    x = x.reshape(bsz * t, d)
    conv_new, shift_new, wkv_new, memk_new, memv_new = [], [], [], [], []
    win_new = None
    for l in range(DEPTH):
        x = _ffn(l, x, P['ffn1_norm'], P['ffn1_w_gate'], P['ffn1_w_up'], P['ffn1_w_down'])

        cols = _norm_proj(l, x, P['mix_norm'], P['w_in'], (A_COLS, 2 * ATTN_W) + (GROUP_W,) * ng + (C_COLS,))
        cols = [c.reshape(bsz, t, c.shape[1]) for c in cols]
        cols_a, qk, vs, cols_c = cols[0], cols[1], cols[2:2 + ng], cols[-1]

        hist = jnp.zeros((bsz, CONV_HIST, CONV_CH), F32) if prompt else conv_cache[l]
        y_a, conv_state = _conv_mixer(l, cols_a, hist, P['conv_w'], P['conv_b'], P['conv_ln_g'], P['conv_ln_b'])
        conv_new.append(conv_state)

        if prompt:
            y_bs, win_new = _attn_prompt(l, qk, vs, cos_t, sin_t, win_new)
            y_bs = list(y_bs)
        else:
            y_b, win_new = _attn_decode(l, _pad_rows(qk, tp), [_pad_rows(v, tp) for v in vs],
                                        cos_t, sin_t, caches, win_new, t)
            y_bs = [y_b[:, :t]]

        if prompt:
            prev = jnp.zeros((bsz, 1, C_COLS), F32)
            s0 = jnp.zeros((bsz, RWKV_HEADS, HEAD_DIM, HEAD_DIM), F32)
            y_c, s_new = _rwkv(l, cols_c, prev, s0, rwkv_params)
        else:
            y_c, s_new = _rwkv(l, _pad_rows(cols_c, tp), shift_cache[l][:, None, :], wkv_cache[l], rwkv_params,
                               t_valid=None if tp == t else t)
            y_c = y_c[:, :t]
        shift_new.append(cols_c[:, -1])
        wkv_new.append(s_new)

        if prompt:
            nm = mem.shape[1]
            mk, mv = _norm_proj(l, mem.reshape(bsz * nm, d), P['mem_norm'], P['xattn_w_kv'], (d, d))
            mk, mv = mk.reshape(bsz, nm, d), mv.reshape(bsz, nm, d)
            memk_new.append(mk.reshape(bsz, nm, XATTN_HEADS, XATTN_HEAD_DIM))
            memv_new.append(mv.reshape(bsz, nm, XATTN_HEADS, XATTN_HEAD_DIM))
        else:
            nm = mem_k_cache.shape[2]
            mk = mem_k_cache[l].reshape(bsz, nm, d)
            mv = mem_v_cache[l].reshape(bsz, nm, d)
        x = _xattn(l, x.reshape(bsz, t, d), [y_a] + y_bs + [y_c], P['w_out'],
                   P['xattn_norm'], P['xattn_w_q'], P['xattn_w_o'], mk, mv).reshape(bsz * t, d)

        x = _ffn(l, x, P['ffn2_norm'], P['ffn2_w_gate'], P['ffn2_w_up'], P['ffn2_w_down'])

    y = _final_norm(x, P['final_norm']).reshape(bsz, t, d)
    stack = jnp.stack
    outs = [y, stack(conv_new)]
    for g in range(ng):
        outs += [_window_unlayout(win_new[g]), _window_unlayout(win_new[ng + g])]
    outs += [stack(shift_new), stack(wkv_new)]
    if prompt:
        outs += [stack(memk_new), stack(memv_new)]
    return outs


_MATMUL_WEIGHTS = ('ffn1_w_gate', 'ffn1_w_up', 'ffn1_w_down', 'w_in', 'w_out', 'rwkv_w_up', 'rwkv_a_up',
                   'rwkv_g_up', 'xattn_w_q', 'xattn_w_kv', 'xattn_w_o', 'ffn2_w_gate', 'ffn2_w_up',
                   'ffn2_w_down')
_ROW_VECTORS = ('ffn1_norm', 'mix_norm', 'conv_b', 'conv_ln_g', 'conv_ln_b', 'rwkv_shift_mix', 'rwkv_w0',
                'rwkv_a0', 'rwkv_k_k', 'rwkv_k_a', 'rwkv_r_k', 'rwkv_lnx_g', 'rwkv_lnx_b', 'xattn_norm',
                'mem_norm', 'ffn2_norm')


def kernel(x_prompt, x_sample, cache_conv, cache_win1_k, cache_win1_v, cache_win2_k, cache_win2_v,
           cache_win3_k, cache_win3_v, state_shift, state_wkv, cache_mem_k, cache_mem_v, mem_prompt,
           ffn1_norm, ffn1_w_gate, ffn1_w_up, ffn1_w_down, mix_norm, w_in, w_out,
           conv_w, conv_b, conv_ln_g, conv_ln_b,
           rwkv_shift_mix, rwkv_w0, rwkv_w_up, rwkv_a0, rwkv_a_up, rwkv_g_up, rwkv_k_k, rwkv_k_a,
           rwkv_r_k, rwkv_lnx_g, rwkv_lnx_b,
           xattn_norm, mem_norm, xattn_w_q, xattn_w_kv, xattn_w_o,
           ffn2_norm, ffn2_w_gate, ffn2_w_up, ffn2_w_down, final_norm):
    P = dict(ffn1_norm=ffn1_norm, ffn1_w_gate=ffn1_w_gate, ffn1_w_up=ffn1_w_up, ffn1_w_down=ffn1_w_down,
             mix_norm=mix_norm, w_in=w_in, w_out=w_out,
             conv_w=conv_w, conv_b=conv_b, conv_ln_g=conv_ln_g, conv_ln_b=conv_ln_b,
             rwkv_shift_mix=rwkv_shift_mix, rwkv_w0=rwkv_w0, rwkv_w_up=rwkv_w_up, rwkv_a0=rwkv_a0,
             rwkv_a_up=rwkv_a_up, rwkv_g_up=rwkv_g_up, rwkv_k_k=rwkv_k_k, rwkv_k_a=rwkv_k_a,
             rwkv_r_k=rwkv_r_k, rwkv_lnx_g=rwkv_lnx_g, rwkv_lnx_b=rwkv_lnx_b,
             xattn_norm=xattn_norm, mem_norm=mem_norm, xattn_w_q=xattn_w_q, xattn_w_kv=xattn_w_kv,
             xattn_w_o=xattn_w_o, ffn2_norm=ffn2_norm, ffn2_w_gate=ffn2_w_gate, ffn2_w_up=ffn2_w_up,
             ffn2_w_down=ffn2_w_down, final_norm=final_norm)
    for name in _MATMUL_WEIGHTS:
        P[name] = P[name].astype(BF16)
    for name in _ROW_VECTORS:
        P[name] = P[name].reshape(DEPTH, 1, -1)

    pos_p = jnp.arange(x_prompt.shape[1], dtype=jnp.int32)
    outs_p = _trunk(x_prompt, pos_p, P, mem_prompt, None, None, None, None, None, None)

    pos_s = PAST_LEN + jnp.arange(x_sample.shape[1], dtype=jnp.int32)
    win_caches = [(cache_win1_k, cache_win1_v), (cache_win2_k, cache_win2_v), (cache_win3_k, cache_win3_v)]
    outs_s = _trunk(x_sample, pos_s, P, None, cache_mem_k, cache_mem_v, cache_conv, win_caches,
                    state_shift, state_wkv)

    return (outs_p[0], outs_s[0], *outs_p[1:], *outs_s[1:])
```
